```python
import jax, jax.numpy as jnp
from jax import lax
import numpy as np

D_MODEL = 1024
BATCH = 4
SEQ = 8192
DEPTH = 2

CHUNK = 64
N_MIXERS = 2
N_GLA_LAYERS = (DEPTH + N_MIXERS - 1) // N_MIXERS
N_MLSTM_LAYERS = DEPTH // N_MIXERS
EPS = 1e-6

GLA_HEADS = 4
GLA_HQK = D_MODEL // 2
GLA_HV = D_MODEL
GLA_DK = GLA_HQK // GLA_HEADS
GLA_DV = GLA_HV // GLA_HEADS
GLA_GATE_RANK = 16
GLA_GATE_TAU = 16.0
GLA_IN = 2 * GLA_HQK + 2 * GLA_HV + GLA_GATE_RANK

MLSTM_HEADS = 4
MLSTM_INNER = 2 * D_MODEL
MLSTM_DH = MLSTM_INNER // MLSTM_HEADS
CONV_W = 4
QKV_BLOCK = 4
MLSTM_NB = MLSTM_INNER // QKV_BLOCK
MLSTM_IN = 2 * MLSTM_INNER + 2 * MLSTM_HEADS

PEER_HEADS = 8
N_KEYS = 128
N_EXPERTS = N_KEYS * N_KEYS
PEER_TOPK = 16
PEER_DQ = 256
PEER_BLOCK = 128

kernel_name = "hybrid_gla_mlstm_peer_chunk_causal"


def rmsnorm(x, g):
    xf = x.astype(jnp.float32)
    y = xf * lax.rsqrt(jnp.mean(xf * xf, axis=-1, keepdims=True) + EPS)
    return (y * g.astype(jnp.float32)).astype(x.dtype)


def to_chunks(t, heads):
    b, s, _ = t.shape
    return t.reshape(b, s // CHUNK, CHUNK, heads, -1).transpose(1, 0, 3, 2, 4)


def from_chunks(t):
    nc, b, h, c, d = t.shape
    return t.transpose(1, 0, 3, 2, 4).reshape(b, nc * c, h * d)


def gate_chunks(t):
    b, s, h = t.shape
    return t.reshape(b, s // CHUNK, CHUNK, h).transpose(1, 0, 3, 2)


def gla_mixer(xn, w_in, w_gate_up, b_gate, g_head, w_out):
    dt = xn.dtype
    b, s, _ = xn.shape
    p = xn.astype(jnp.float32) @ w_in
    q, k, v, r, glow = jnp.split(
        p, [GLA_HQK, 2 * GLA_HQK, 2 * GLA_HQK + GLA_HV, 2 * GLA_HQK + 2 * GLA_HV], axis=-1)
    log_a = jax.nn.log_sigmoid(glow @ w_gate_up + b_gate) / GLA_GATE_TAU
    q = q * (GLA_DK ** -0.5)
    qc = to_chunks(q, GLA_HEADS)
    kc = to_chunks(k, GLA_HEADS)
    vc = to_chunks(v, GLA_HEADS)
    cum = jnp.cumsum(to_chunks(log_a, GLA_HEADS), axis=3)
    tot = cum[:, :, :, -1]
    k_dec = kc * jnp.exp(tot[:, :, :, None, :] - cum)

    def step(state, xs):
        q_c, k_c, v_c, tot_c = xs
        state = jnp.exp(tot_c)[..., None] * state + jnp.einsum('bhck,bhcv->bhkv', k_c, v_c)
        o = jnp.einsum('bhck,bhkv->bhcv', q_c, state)
        return state, o

    s0 = jnp.zeros((b, GLA_HEADS, GLA_DK, GLA_DV), jnp.float32)
    _, o = lax.scan(step, s0, (qc, k_dec, vc, tot))
    o = from_chunks(o).reshape(b, s, GLA_HEADS, GLA_DV)
    o = o * lax.rsqrt(jnp.mean(o * o, axis=-1, keepdims=True) + EPS)
    o = o.reshape(b, s, GLA_HV) * g_head * jax.nn.silu(r)
    return (o @ w_out).astype(dt)


def blockdiag(x, w):
    nb, bs, _ = w.shape
    xb = x.reshape(x.shape[:-1] + (nb, bs))
    return jnp.einsum('...ni,nio->...no', xb, w).reshape(x.shape[:-1] + (nb * bs,))


def mlstm_mixer(xn, w_in, b_i, b_f, conv_w, conv_b, w_q, w_k, w_v, skip, g_head, w_out):
    dt = xn.dtype
    b, s, _ = xn.shape
    p = xn.astype(jnp.float32) @ w_in
    x_m, z, i_pre, f_pre = jnp.split(
        p, [MLSTM_INNER, 2 * MLSTM_INNER, 2 * MLSTM_INNER + MLSTM_HEADS], axis=-1)
    x_conv = lax.conv_general_dilated(
        x_m, conv_w.astype(jnp.float32)[:, None, :], window_strides=(1,),
        padding=[(CONV_W - 1, 0)], dimension_numbers=('NWC', 'WIO', 'NWC'),
        feature_group_count=MLSTM_INNER)
    x_c = jax.nn.silu(x_conv + conv_b)
    q = blockdiag(x_c, w_q)
    k = blockdiag(x_c, w_k) * (MLSTM_DH ** -0.5)
    v = blockdiag(x_m, w_v)
    i_log = gate_chunks(i_pre + b_i)
    log_f = gate_chunks(jax.nn.log_sigmoid(f_pre + b_f))
    cumf = jnp.cumsum(log_f, axis=-1)
    tot = cumf[..., -1]
    w_log = tot[..., None] - cumf + i_log
    qc = to_chunks(q, MLSTM_HEADS)
    kc = to_chunks(k, MLSTM_HEADS)
    vc = to_chunks(v, MLSTM_HEADS)

    def step(carry, xs):
        c_st, n_st, m_st = carry
        q_c, k_c, v_c, w_c, tot_c = xs
        m_new = jnp.maximum(tot_c + m_st, jnp.max(w_c, axis=-1))
        a_prev = jnp.exp(tot_c + m_st - m_new)
        kw = k_c * jnp.exp(w_c - m_new[..., None])[..., None]
        c_new = a_prev[..., None, None] * c_st + jnp.einsum('bhck,bhcv->bhkv', kw, v_c)
        n_new = a_prev[..., None] * n_st + jnp.sum(kw, axis=2)
        num = jnp.einsum('bhck,bhkv->bhcv', q_c, c_new)
        den = jnp.maximum(jnp.abs(jnp.einsum('bhck,bhk->bhc', q_c, n_new)),
                          jnp.exp(-m_new)[..., None])
        return (c_new, n_new, m_new), num / den[..., None]

    c0 = jnp.zeros((b, MLSTM_HEADS, MLSTM_DH, MLSTM_DH), jnp.float32)
    n0 = jnp.zeros((b, MLSTM_HEADS, MLSTM_DH), jnp.float32)
    m0 = jnp.zeros((b, MLSTM_HEADS), jnp.float32)
    _, h = lax.scan(step, (c0, n0, m0), (qc, kc, vc, w_log, tot))
    h = from_chunks(h).reshape(b, s, MLSTM_HEADS, MLSTM_DH)
    mu = jnp.mean(h, axis=-1, keepdims=True)
    var = jnp.mean(jnp.square(h - mu), axis=-1, keepdims=True)
    h = ((h - mu) * lax.rsqrt(var + EPS)).reshape(b, s, MLSTM_INNER) * g_head
    h = (h + skip * x_c) * jax.nn.silu(z)
    return (h @ w_out).astype(dt)


def peer_ffn(xn, w_query, sub_keys, u_tab, v_tab):
    dt = xn.dtype
    b, s, d = xn.shape
    t = b * s
    xt = xn.reshape(t, d)
    q = (xt @ w_query).astype(jnp.float32).reshape(t, PEER_HEADS, 2, PEER_DQ // 2)
    sc = jnp.einsum('thpd,hpnd->thpn', q, sub_keys.astype(jnp.float32))
    s_top, i_top = lax.top_k(sc, PEER_TOPK)
    cand = (s_top[:, :, 0, :, None] + s_top[:, :, 1, None, :]).reshape(t, PEER_HEADS, -1)
    cand_idx = (i_top[:, :, 0, :, None] * N_KEYS + i_top[:, :, 1, None, :]).reshape(t, PEER_HEADS, -1)
    best, pos = lax.top_k(cand, PEER_TOPK)
    idx = jnp.take_along_axis(cand_idx, pos, axis=-1)
    gates = jax.nn.softmax(best, axis=-1)

    def block(args):
        x_b, idx_b, g_b = args
        u = jnp.take(u_tab, idx_b, axis=0)
        act = jax.nn.gelu(jnp.einsum('td,thkd->thk', x_b, u).astype(jnp.float32), approximate=False)
        vsel = jnp.take(v_tab, idx_b, axis=0)
        return jnp.einsum('thk,thkd->td', g_b * act, vsel).astype(dt)

    nblk = t // PEER_BLOCK
    y = lax.map(block, (xt.reshape(nblk, PEER_BLOCK, d),
                        idx.reshape(nblk, PEER_BLOCK, PEER_HEADS, PEER_TOPK),
                        gates.reshape(nblk, PEER_BLOCK, PEER_HEADS, PEER_TOPK)))
    return y.reshape(b, s, d)


def setup_inputs(seed: int = 0) -> dict:
    key = jax.random.key(seed)
    ks = jax.random.split(key, 26)

    def nrm(k, shape, scale):
        return jax.random.normal(k, shape, jnp.float32) * scale

    LG, LM = N_GLA_LAYERS, N_MLSTM_LAYERS
    return {
        "x": nrm(ks[0], (BATCH, SEQ, D_MODEL), 1.0),
        "norm_mix_g": 1.0 + nrm(ks[1], (DEPTH, D_MODEL), 0.02),
        "gla_w_in": nrm(ks[2], (LG, D_MODEL, GLA_IN), D_MODEL ** -0.5),
        "gla_w_gate_up": nrm(ks[3], (LG, GLA_GATE_RANK, GLA_HQK), GLA_GATE_RANK ** -0.5),
        "gla_b_gate": nrm(ks[4], (LG, GLA_HQK), 0.1),
        "gla_g_head": 1.0 + nrm(ks[5], (LG, GLA_HV), 0.02),
        "gla_w_out": nrm(ks[6], (LG, GLA_HV, D_MODEL), GLA_HV ** -0.5),
        "mlstm_w_in": nrm(ks[7], (LM, D_MODEL, MLSTM_IN), D_MODEL ** -0.5),
        "mlstm_b_i": nrm(ks[8], (LM, MLSTM_HEADS), 0.1),
        "mlstm_b_f": jnp.broadcast_to(jnp.linspace(3.0, 6.0, MLSTM_HEADS, dtype=jnp.float32), (LM, MLSTM_HEADS))
                     + nrm(ks[9], (LM, MLSTM_HEADS), 0.01),
        "mlstm_conv_w": nrm(ks[10], (LM, CONV_W, MLSTM_INNER), CONV_W ** -0.5),
        "mlstm_conv_b": nrm(ks[11], (LM, MLSTM_INNER), 0.02),
        "mlstm_w_q": nrm(ks[12], (LM, MLSTM_NB, QKV_BLOCK, QKV_BLOCK), QKV_BLOCK ** -0.5),
        "mlstm_w_k": nrm(ks[13], (LM, MLSTM_NB, QKV_BLOCK, QKV_BLOCK), QKV_BLOCK ** -0.5),
        "mlstm_w_v": nrm(ks[14], (LM, MLSTM_NB, QKV_BLOCK, QKV_BLOCK), QKV_BLOCK ** -0.5),
        "mlstm_skip": 1.0 + nrm(ks[15], (LM, MLSTM_INNER), 0.02),
        "mlstm_g_head": 1.0 + nrm(ks[16], (LM, MLSTM_INNER), 0.02),
        "mlstm_w_out": nrm(ks[17], (LM, MLSTM_INNER, D_MODEL), MLSTM_INNER ** -0.5),
        "norm_ffn_g": 1.0 + nrm(ks[18], (DEPTH, D_MODEL), 0.02),
        "peer_w_query": nrm(ks[19], (DEPTH, D_MODEL, PEER_HEADS * PEER_DQ), D_MODEL ** -0.5),
        "peer_sub_keys": nrm(ks[20], (DEPTH, PEER_HEADS, 2, N_KEYS, PEER_DQ // 2), (PEER_DQ // 2) ** -0.5),
        "peer_u": nrm(ks[21], (DEPTH, N_EXPERTS, D_MODEL), D_MODEL ** -0.5),
        "peer_v": nrm(ks[22], (DEPTH, N_EXPERTS, D_MODEL), 0.1),
        "norm_final_g": 1.0 + nrm(ks[23], (D_MODEL,), 0.02),
    }


def reference(x, norm_mix_g, gla_w_in, gla_w_gate_up, gla_b_gate, gla_g_head, gla_w_out,
              mlstm_w_in, mlstm_b_i, mlstm_b_f, mlstm_conv_w, mlstm_conv_b, mlstm_w_q, mlstm_w_k,
              mlstm_w_v, mlstm_skip, mlstm_g_head, mlstm_w_out, norm_ffn_g, peer_w_query,
              peer_sub_keys, peer_u, peer_v, norm_final_g):
    for i in range(DEPTH):
        xn = rmsnorm(x, norm_mix_g[i])
        j = i // N_MIXERS
        if i % N_MIXERS == 0:
            mix = gla_mixer(xn, gla_w_in[j], gla_w_gate_up[j], gla_b_gate[j], gla_g_head[j], gla_w_out[j])
        else:
            mix = mlstm_mixer(xn, mlstm_w_in[j], mlstm_b_i[j], mlstm_b_f[j], mlstm_conv_w[j],
                              mlstm_conv_b[j], mlstm_w_q[j], mlstm_w_k[j], mlstm_w_v[j],
                              mlstm_skip[j], mlstm_g_head[j], mlstm_w_out[j])
        x = x + mix
        x = x + peer_ffn(rmsnorm(x, norm_ffn_g[i]), peer_w_query[i], peer_sub_keys[i],
                         peer_u[i], peer_v[i])
    return rmsnorm(x, norm_final_g)
```

```python
import functools
import math

import jax
import jax.numpy as jnp
from jax import lax
from jax.experimental import pallas as pl
from jax.experimental.pallas import tpu as pltpu

F32 = jnp.float32
BF16 = jnp.bfloat16

EPS = 1e-6
CHUNK = 64
LANES = 128
SUBLANES = 8

GLA_HEADS = 4
GLA_GATE_RANK = 16
GLA_GATE_TAU = 16.0

MLSTM_HEADS = 4
CONV_W = 4
QKV_BLOCK = 4
BD_TILE = 256

PEER_HEADS = 8
N_KEYS = 128
PEER_TOPK = 16
ROWS_PER_EXPERT = 4

MIX_TM = 256
ROUTE_TM = 256
PEER_TB = 64
VMEM_LIMIT = 56 * 1024 * 1024


def _split3(x):
    h1 = x.astype(BF16)
    r1 = x - h1.astype(F32)
    h2 = r1.astype(BF16)
    h3 = (r1 - h2.astype(F32)).astype(BF16)
    return h1, h2, h3


def _dot(a, b):
    return jnp.dot(a, b, preferred_element_type=F32)


def _dot_nt(a, b):
    return lax.dot_general(a, b, (((1,), (1,)), ((), ())), preferred_element_type=F32)


def _dot_tn(a, b):
    return lax.dot_general(a, b, (((0,), (0,)), ((), ())), preferred_element_type=F32)


def _exact_left_dot(m_bf16, x):
    h1, h2, h3 = _split3(x)
    return _dot(m_bf16, h1) + _dot(m_bf16, h2) + _dot(m_bf16, h3)


def _rmsnorm(x, g):
    return x * lax.rsqrt(jnp.mean(x * x, axis=-1, keepdims=True) + EPS) * g


def _log_sigmoid(z):
    return jnp.minimum(z, 0.0) - jnp.log1p(jnp.exp(-jnp.abs(z)))


def _silu(z):
    return z * (1.0 / (1.0 + jnp.exp(-z)))


def _const_spec(shape):
    nd = len(shape)
    return pl.BlockSpec(shape, lambda *_: (0,) * nd, pipeline_mode=pl.Buffered(1))


def _gla_kernel(x_ref, g_ref, wall_ref, wgu_ref, bg_ref, gh_ref, wout_ref, tri_ref,
                o_ref, st_ref, p_ref, la_ref, ob_ref, *, hqk, hv):
    dk = hqk // GLA_HEADS
    dv = hv // GLA_HEADS
    tm = x_ref.shape[1]

    @pl.when(pl.program_id(1) == 0)
    def _():
        st_ref[...] = jnp.zeros_like(st_ref)

    x = x_ref[0]
    xn = _rmsnorm(x, g_ref[...]).astype(BF16)
    p_ref[...] = _dot(xn, wall_ref[...])

    o_q, o_k, o_v, o_r, o_g = 0, hqk, 2 * hqk, 2 * hqk + hv, 2 * hqk + 2 * hv
    glow = p_ref[:, o_g:o_g + LANES]
    g1, g2, g3 = _split3(glow)
    w1, w2, w3 = wgu_ref[0], wgu_ref[1], wgu_ref[2]
    z = (_dot(g1, w1) + _dot(g1, w2) + _dot(g2, w1)
         + _dot(g1, w3) + _dot(g2, w2) + _dot(g3, w1)) + bg_ref[...]
    la_ref[...] = _log_sigmoid(z) * (1.0 / GLA_GATE_TAU)

    tri = tri_ref[...]
    for c in range(tm // CHUNK):
        r0 = c * CHUNK
        for h in range(GLA_HEADS):
            la = la_ref[r0:r0 + CHUNK, h * dk:(h + 1) * dk]
            cum = _exact_left_dot(tri, la)
            tot = cum[CHUNK - 1:CHUNK, :]
            kd = p_ref[r0:r0 + CHUNK, o_k + h * dk:o_k + (h + 1) * dk] * jnp.exp(tot - cum)
            vc = p_ref[r0:r0 + CHUNK, o_v + h * dv:o_v + (h + 1) * dv]
            st = st_ref[h] * jnp.exp(tot) + _dot_tn(vc.astype(BF16), kd.astype(BF16))
            st_ref[h] = st
            qc = p_ref[r0:r0 + CHUNK, o_q + h * dk:o_q + (h + 1) * dk] * (dk ** -0.5)
            ob_ref[r0:r0 + CHUNK, h * dv:(h + 1) * dv] = _dot_nt(qc.astype(BF16), st.astype(BF16))

    for h in range(GLA_HEADS):
        o = ob_ref[:, h * dv:(h + 1) * dv]
        o = o * lax.rsqrt(jnp.mean(o * o, axis=-1, keepdims=True) + EPS)
        r = p_ref[:, o_r + h * dv:o_r + (h + 1) * dv]
        ob_ref[:, h * dv:(h + 1) * dv] = o * gh_ref[:, h * dv:(h + 1) * dv] * _silu(r)
    o_ref[0] = x + _dot(ob_ref[...].astype(BF16), wout_ref[...])


def _gla_layer(x, g_mix, w_in, w_gate_up, b_gate, g_head, w_out):
    b, s, d = x.shape
    hqk = w_gate_up.shape[1]
    hv = g_head.shape[0]
    pad = LANES - GLA_GATE_RANK
    wall = jnp.pad(w_in, ((0, 0), (0, pad))).astype(BF16)
    wgu = jnp.stack(_split3(jnp.pad(w_gate_up, ((0, pad), (0, 0)))))
    tri = jnp.tril(jnp.ones((CHUNK, CHUNK), F32)).astype(BF16)
    ncol = wall.shape[1]
    tm = MIX_TM
    kern = functools.partial(_gla_kernel, hqk=hqk, hv=hv)
    return pl.pallas_call(
        kern,
        grid=(b, s // tm),
        in_specs=[
            pl.BlockSpec((1, tm, d), lambda i, j: (i, j, 0)),
            _const_spec((1, d)),
            _const_spec((d, ncol)),
            _const_spec((3, LANES, hqk)),
            _const_spec((1, hqk)),
            _const_spec((1, hv)),
            _const_spec((hv, d)),
            _const_spec((CHUNK, CHUNK)),
        ],
        out_specs=pl.BlockSpec((1, tm, d), lambda i, j: (i, j, 0)),
        out_shape=jax.ShapeDtypeStruct((b, s, d), F32),
        scratch_shapes=[
            pltpu.VMEM((GLA_HEADS, hv // GLA_HEADS, hqk // GLA_HEADS), F32),
            pltpu.VMEM((tm, ncol), F32),
            pltpu.VMEM((tm, hqk), F32),
            pltpu.VMEM((tm, hv), F32),
        ],
        compiler_params=pltpu.CompilerParams(
            dimension_semantics=("arbitrary", "arbitrary"), vmem_limit_bytes=VMEM_LIMIT),
        name="gla_layer",
    )(x, g_mix.reshape(1, d), wall, wgu, b_gate.reshape(1, hqk), g_head.reshape(1, hv),
      w_out.astype(BF16), tri)


def _mlstm_kernel(x_ref, g_ref, wall_ref, bi_ref, bf_ref, cw_ref, cb_ref, wq_ref, wk_ref, wv_ref,
                  skip_ref, gh_ref, wout_ref, tri_ref,
                  o_ref, ct_ref, n_ref, m_ref, tail_ref, p_ref, xc_ref, q_ref, k_ref, v_ref, hb_ref,
                  *, inner):
    dh = inner // MLSTM_HEADS
    tm = x_ref.shape[1]

    @pl.when(pl.program_id(1) == 0)
    def _():
        ct_ref[...] = jnp.zeros_like(ct_ref)
        n_ref[...] = jnp.zeros_like(n_ref)
        m_ref[...] = jnp.zeros_like(m_ref)
        tail_ref[...] = jnp.zeros_like(tail_ref)

    x = x_ref[0]
    xn = _rmsnorm(x, g_ref[...]).astype(BF16)
    p_ref[...] = _dot(xn, wall_ref[...])

    xm = p_ref[:, 0:inner]
    ext = jnp.concatenate([tail_ref[...], xm], axis=0)
    conv = cb_ref[...]
    for w in range(CONV_W):
        s0 = SUBLANES - (CONV_W - 1) + w
        conv = conv + cw_ref[w:w + 1, :] * ext[s0:s0 + tm, :]
    tail_ref[...] = xm[tm - SUBLANES:tm, :]
    xc = _silu(conv)
    xc_ref[...] = xc

    xcb = xc.astype(BF16)
    xmb = xm.astype(BF16)
    for t in range(inner // BD_TILE):
        sl = slice(t * BD_TILE, (t + 1) * BD_TILE)
        q_ref[:, sl] = _dot(xcb[:, sl], wq_ref[t])
        k_ref[:, sl] = _dot(xcb[:, sl], wk_ref[t]) * (dh ** -0.5)
        v_ref[:, sl] = _dot(xmb[:, sl], wv_ref[t]).astype(BF16)

    o_i = 2 * inner
    o_f = 2 * inner + LANES
    tri = tri_ref[...]
    for c in range(tm // CHUNK):
        r0 = c * CHUNK
        i_log = p_ref[r0:r0 + CHUNK, o_i:o_i + LANES] + bi_ref[...]
        log_f = _log_sigmoid(p_ref[r0:r0 + CHUNK, o_f:o_f + LANES] + bf_ref[...])
        cumf = _exact_left_dot(tri, log_f)
        tot = cumf[CHUNK - 1:CHUNK, :]
        w_log = tot - cumf + i_log
        m_old = m_ref[...]
        m_new = jnp.maximum(tot + m_old, jnp.max(w_log, axis=0, keepdims=True))
        a_prev = jnp.exp(tot + m_old - m_new)
        ew = jnp.exp(w_log - m_new)
        floor = jnp.exp(-m_new)
        m_ref[...] = m_new
        for h in range(MLSTM_HEADS):
            hs = slice(h * dh, (h + 1) * dh)
            a_h = a_prev[:, h:h + 1]
            kw = k_ref[r0:r0 + CHUNK, hs] * ew[:, h:h + 1]
            ct = a_h * ct_ref[h] + _dot_tn(v_ref[r0:r0 + CHUNK, hs], kw.astype(BF16))
            ct_ref[h] = ct
            n_new = a_h * n_ref[h] + jnp.sum(kw, axis=0, keepdims=True)
            n_ref[h] = n_new
            qc = q_ref[r0:r0 + CHUNK, hs]
            num = _dot_nt(qc.astype(BF16), ct.astype(BF16))
            den = jnp.maximum(jnp.abs(jnp.sum(qc * n_new, axis=-1, keepdims=True)), floor[:, h:h + 1])
            hb_ref[r0:r0 + CHUNK, hs] = num / den

    for h in range(MLSTM_HEADS):
        hs = slice(h * dh, (h + 1) * dh)
        hv = hb_ref[:, hs]
        mu = jnp.mean(hv, axis=-1, keepdims=True)
        cen = hv - mu
        var = jnp.mean(cen * cen, axis=-1, keepdims=True)
        hn = cen * lax.rsqrt(var + EPS) * gh_ref[:, hs]
        z = p_ref[:, inner + h * dh:inner + (h + 1) * dh]
        hb_ref[:, hs] = (hn + skip_ref[:, hs] * xc_ref[:, hs]) * _silu(z)
    o_ref[0] = x + _dot(hb_ref[...].astype(BF16), wout_ref[...])


def _blockdiag_tiles(w):
    nb, bs, _ = w.shape
    per = BD_TILE // bs
    wt = w.reshape(nb // per, per, bs, bs)
    eye = jnp.eye(per, dtype=w.dtype)
    full = wt[:, :, :, None, :] * eye[None, :, None, :, None]
    return full.reshape(nb // per, BD_TILE, BD_TILE).astype(BF16)


def _mlstm_layer(x, g_mix, w_in, b_i, b_f, conv_w, conv_b, w_q, w_k, w_v, skip, g_head, w_out):
    b, s, d = x.shape
    inner = conv_w.shape[1]
    nh = MLSTM_HEADS
    pad = LANES - nh
    wall = jnp.concatenate([
        w_in[:, :2 * inner],
        jnp.pad(w_in[:, 2 * inner:2 * inner + nh], ((0, 0), (0, pad))),
        jnp.pad(w_in[:, 2 * inner + nh:], ((0, 0), (0, pad))),
    ], axis=1).astype(BF16)
    ncol = wall.shape[1]
    bi = jnp.pad(b_i, (0, pad)).reshape(1, LANES)
    bf = jnp.pad(b_f, (0, pad)).reshape(1, LANES)
    tri = jnp.tril(jnp.ones((CHUNK, CHUNK), F32)).astype(BF16)
    nt = inner // BD_TILE
    tm = MIX_TM
    dh = inner // nh
    kern = functools.partial(_mlstm_kernel, inner=inner)
    return pl.pallas_call(
        kern,
        grid=(b, s // tm),
        in_specs=[
            pl.BlockSpec((1, tm, d), lambda i, j: (i, j, 0)),
            _const_spec((1, d)),
            _const_spec((d, ncol)),
            _const_spec((1, LANES)),
            _const_spec((1, LANES)),
            _const_spec((CONV_W, inner)),
            _const_spec((1, inner)),
            _const_spec((nt, BD_TILE, BD_TILE)),
            _const_spec((nt, BD_TILE, BD_TILE)),
            _const_spec((nt, BD_TILE, BD_TILE)),
            _const_spec((1, inner)),
            _const_spec((1, inner)),
            _const_spec((inner, d)),
            _const_spec((CHUNK, CHUNK)),
        ],
        out_specs=pl.BlockSpec((1, tm, d), lambda i, j: (i, j, 0)),
        out_shape=jax.ShapeDtypeStruct((b, s, d), F32),
        scratch_shapes=[
            pltpu.VMEM((nh, dh, dh), F32),
            pltpu.VMEM((nh, 1, dh), F32),
            pltpu.VMEM((1, LANES), F32),
            pltpu.VMEM((SUBLANES, inner), F32),
            pltpu.VMEM((tm, ncol), F32),
            pltpu.VMEM((tm, inner), F32),
            pltpu.VMEM((tm, inner), F32),
            pltpu.VMEM((tm, inner), F32),
            pltpu.VMEM((tm, inner), BF16),
            pltpu.VMEM((tm, inner), F32),
        ],
        compiler_params=pltpu.CompilerParams(
            dimension_semantics=("arbitrary", "arbitrary"), vmem_limit_bytes=VMEM_LIMIT),
        name="mlstm_layer",
    )(x, g_mix.reshape(1, d), wall, bi, bf, conv_w, conv_b.reshape(1, inner),
      _blockdiag_tiles(w_q), _blockdiag_tiles(w_k), _blockdiag_tiles(w_v),
      skip.reshape(1, inner), g_head.reshape(1, inner), w_out.astype(BF16), tri)


def _route_kernel(x_ref, g_ref, wq_ref, keys_ref, offs_ref, gates_ref, xlo_ref, xhi_ref, q_ref):
    tm, d = x_ref.shape
    half = d // 2
    nk = N_KEYS
    kk = PEER_TOPK
    xn = _rmsnorm(x_ref[...], g_ref[...])
    xlo_ref[...] = jnp.concatenate([xn[:, :half], xn[:, :half]], axis=1)
    xhi_ref[...] = jnp.concatenate([xn[:, half:], xn[:, half:]], axis=1)
    q_ref[...] = _dot(xn.astype(BF16), wq_ref[...])

    lane = lax.broadcasted_iota(jnp.int32, (tm, nk), 1)
    lanef = lane.astype(F32)
    lane2 = lax.broadcasted_iota(jnp.int32, (tm, kk * kk), 1)
    grp_a = lane2 // kk
    grp_b = lane2 % kk
    pos_key = lane2.astype(F32) * float(nk * nk)
    neg = jnp.float32(-jnp.inf)
    big = jnp.float32(3e38)

    idx_acc = jnp.zeros((tm, nk), F32)
    g_acc = jnp.zeros((tm, nk), F32)
    for h in range(PEER_HEADS):
        sa0 = _dot(q_ref[:, (2 * h) * nk:(2 * h + 1) * nk].astype(BF16), keys_ref[2 * h])
        sb0 = _dot(q_ref[:, (2 * h + 1) * nk:(2 * h + 2) * nk].astype(BF16), keys_ref[2 * h + 1])

        def stage1(i, carry):
            sa, sb, va, vb, ia_, ib_ = carry
            ma = jnp.max(sa, axis=-1, keepdims=True)
            ja = jnp.argmax(sa, axis=-1, keepdims=True).astype(F32)
            mb = jnp.max(sb, axis=-1, keepdims=True)
            jb = jnp.argmax(sb, axis=-1, keepdims=True).astype(F32)
            sa = jnp.where(lanef == ja, neg, sa)
            sb = jnp.where(lanef == jb, neg, sb)
            sel_a = grp_a == i
            sel_b = grp_b == i
            va = jnp.where(sel_a, ma, va)
            ia_ = jnp.where(sel_a, ja, ia_)
            vb = jnp.where(sel_b, mb, vb)
            ib_ = jnp.where(sel_b, jb, ib_)
            return sa, sb, va, vb, ia_, ib_

        zc = jnp.zeros((tm, kk * kk), F32)
        _, _, va, vb, ia_, ib_ = lax.fori_loop(0, kk, stage1, (sa0, sb0, zc, zc, zc, zc))
        cand0 = va + vb
        key = pos_key + ia_ * float(nk) + ib_

        def stage2(i, carry):
            cand, idx_a, e_a, ssum, m0 = carry
            m = jnp.max(cand, axis=-1, keepdims=True)
            sel = jnp.min(jnp.where(cand == m, key, big), axis=-1, keepdims=True)
            cand = jnp.where(key == sel, neg, cand)
            expert = sel - jnp.floor(sel * (1.0 / (nk * nk))) * float(nk * nk)
            m0 = jnp.where(i == 0, m, m0)
            e = jnp.exp(m - m0)
            hit = lane == (h * kk + i)
            idx_a = jnp.where(hit, expert, idx_a)
            e_a = jnp.where(hit, e, e_a)
            return cand, idx_a, e_a, ssum + e, m0

        z1 = jnp.zeros((tm, 1), F32)
        _, idx_acc, e_acc, ssum, _ = lax.fori_loop(
            0, kk, stage2, (cand0, idx_acc, jnp.zeros((tm, nk), F32), z1, z1))
        g_acc = jnp.where((lane // kk) == h, e_acc / ssum, g_acc)

    adj = jnp.where((lane % SUBLANES) < ROWS_PER_EXPERT, ROWS_PER_EXPERT, 0)
    offs_ref[...] = idx_acc.astype(jnp.int32) * ROWS_PER_EXPERT + adj
    gates_ref[...] = g_acc


def _peer_route(x2, g, w_query, sub_keys):
    t, d = x2.shape
    nq = w_query.shape[1]
    keys = jnp.swapaxes(sub_keys.reshape(PEER_HEADS * 2, N_KEYS, -1), 1, 2).astype(BF16)
    tm = ROUTE_TM
    return pl.pallas_call(
        _route_kernel,
        grid=(t // tm,),
        in_specs=[
            pl.BlockSpec((tm, d), lambda i: (i, 0)),
            _const_spec((1, d)),
            _const_spec((d, nq)),
            _const_spec(keys.shape),
        ],
        out_specs=[
            pl.BlockSpec((tm, LANES), lambda i: (i, 0)),
            pl.BlockSpec((tm, LANES), lambda i: (i, 0)),
            pl.BlockSpec((tm, d), lambda i: (i, 0)),
            pl.BlockSpec((tm, d), lambda i: (i, 0)),
        ],
        out_shape=[
            jax.ShapeDtypeStruct((t, LANES), jnp.int32),
            jax.ShapeDtypeStruct((t, LANES), F32),
            jax.ShapeDtypeStruct((t, d), F32),
            jax.ShapeDtypeStruct((t, d), F32),
        ],
        scratch_shapes=[pltpu.VMEM((tm, nq), F32)],
        compiler_params=pltpu.CompilerParams(
            dimension_semantics=("arbitrary",), vmem_limit_bytes=VMEM_LIMIT),
        name="peer_route",
    )(x2, g.reshape(1, d), w_query.astype(BF16), keys)


def _pack_table(tab):
    e, d = tab.shape
    bits = lax.bitcast_convert_type(tab.astype(BF16), jnp.uint16).astype(jnp.uint32)
    words = (bits[:, d // 2:] << 16) | bits[:, :d // 2]
    words = words.reshape(e * ROWS_PER_EXPERT, LANES)
    return jnp.pad(words, ((ROWS_PER_EXPERT, ROWS_PER_EXPERT), (0, 0)))


def _load_pair(tab_ref, off_a, off_b, lo_half):
    ta = tab_ref[pl.ds(off_a, SUBLANES), :]
    tb = tab_ref[pl.ds(off_b, SUBLANES), :]
    mt = jnp.where(lo_half, ta, tb)
    lo = pltpu.bitcast(mt << 16, F32)
    hi = pltpu.bitcast(mt & jnp.uint32(0xFFFF0000), F32)
    return lo, hi


_PAIR_SLOTS = ((0, 4), (2, 6), (1, 5), (3, 7))


def _act_kernel(offs_ref, xlo_ref, xhi_ref, gates_ref, tab_ref, w_ref, rbuf_ref):
    tb = xlo_ref.shape[0]
    nslot = gates_ref.shape[1]
    sub = lax.broadcasted_iota(jnp.int32, (SUBLANES, LANES), 0)
    lo_half = sub < ROWS_PER_EXPERT
    m2 = (sub % 4) < 2
    m1 = (sub % 2) == 0

    def fold2(a, b_):
        return jnp.where(m2, a + pltpu.roll(a, 6, 0), b_ + pltpu.roll(b_, 2, 0))

    def fold1(a, b_):
        return jnp.where(m1, a + pltpu.roll(a, 7, 0), b_ + pltpu.roll(b_, 1, 0))

    def token(t, carry):
        xlo = xlo_ref[t]
        xhi = xhi_ref[t]
        for g in range(nslot // SUBLANES):
            prods = []
            for sa, sb in _PAIR_SLOTS:
                lo, hi = _load_pair(tab_ref, offs_ref[t, g * SUBLANES + sa],
                                    offs_ref[t, g * SUBLANES + sb], lo_half)
                prods.append(lo * xlo + hi * xhi)
            r = fold1(fold2(prods[0], prods[1]), fold2(prods[2], prods[3]))
            rbuf_ref[pl.ds(pl.multiple_of(t * nslot + g * SUBLANES, SUBLANES), SUBLANES), :] = r
        return carry

    lax.fori_loop(0, tb, token, 0)

    def octet(o, carry):
        acc = jnp.zeros((SUBLANES, LANES), F32)
        for r in range(SUBLANES):
            row0 = pl.multiple_of((o * SUBLANES + r) * nslot, nslot)
            rt = rbuf_ref[pl.ds(row0, nslot), :]
            sel = (sub == r).astype(BF16)
            h1, h2, h3 = _split3(rt)
            acc = acc + _dot_nt(sel, h1) + _dot_nt(sel, h2) + _dot_nt(sel, h3)
        gelu = 0.5 * acc * (1.0 + lax.erf(acc * (1.0 / math.sqrt(2.0))))
        rows = pl.ds(pl.multiple_of(o * SUBLANES, SUBLANES), SUBLANES)
        w_ref[rows, :] = gates_ref[rows, :] * gelu
        return carry

    lax.fori_loop(0, tb // SUBLANES, octet, 0)


def _peer_act(offs, xlo, xhi, gates, tab):
    t = offs.shape[0]
    tb = PEER_TB
    return pl.pallas_call(
        _act_kernel,
        grid=(t // tb,),
        in_specs=[
            pl.BlockSpec((tb, LANES), lambda i: (i, 0), memory_space=pltpu.SMEM),
            pl.BlockSpec((tb, SUBLANES, LANES), lambda i: (i, 0, 0)),
            pl.BlockSpec((tb, SUBLANES, LANES), lambda i: (i, 0, 0)),
            pl.BlockSpec((tb, LANES), lambda i: (i, 0)),
            _const_spec(tab.shape),
        ],
        out_specs=pl.BlockSpec((tb, LANES), lambda i: (i, 0)),
        out_shape=jax.ShapeDtypeStruct((t, LANES), F32),
        scratch_shapes=[pltpu.VMEM((tb * LANES, LANES), F32)],
        compiler_params=pltpu.CompilerParams(
            dimension_semantics=("arbitrary",), vmem_limit_bytes=VMEM_LIMIT),
        name="peer_act",
    )(offs, xlo, xhi, gates, tab)


def _out_kernel(offs_ref, w_ref, x_ref, tab_ref, o_ref):
    tb = x_ref.shape[0]
    nslot = w_ref.shape[1]
    sub = lax.broadcasted_iota(jnp.int32, (SUBLANES, LANES), 0)
    lo_half = sub < ROWS_PER_EXPERT

    def token(t, carry):
        acc_lo = [jnp.zeros((SUBLANES, LANES), F32) for _ in range(2)]
        acc_hi = [jnp.zeros((SUBLANES, LANES), F32) for _ in range(2)]
        n = 0
        for g in range(nslot // SUBLANES):
            for sa, sb in _PAIR_SLOTS:
                ja = g * SUBLANES + sa
                jb = g * SUBLANES + sb
                lo, hi = _load_pair(tab_ref, offs_ref[t, ja], offs_ref[t, jb], lo_half)
                wv = jnp.where(lo_half, w_ref[t, ja], w_ref[t, jb])
                acc_lo[n % 2] = acc_lo[n % 2] + lo * wv
                acc_hi[n % 2] = acc_hi[n % 2] + hi * wv
                n += 1
        ylo = acc_lo[0] + acc_lo[1]
        yhi = acc_hi[0] + acc_hi[1]
        y = jnp.where(lo_half, ylo + pltpu.roll(ylo, 4, 0), yhi + pltpu.roll(yhi, 4, 0))
        o_ref[t] = x_ref[t] + y
        return carry

    lax.fori_loop(0, tb, token, 0)


def _peer_out(offs, w, x3, tab):
    t = offs.shape[0]
    tb = PEER_TB
    return pl.pallas_call(
        _out_kernel,
        grid=(t // tb,),
        in_specs=[
            pl.BlockSpec((tb, LANES), lambda i: (i, 0), memory_space=pltpu.SMEM),
            pl.BlockSpec((tb, LANES), lambda i: (i, 0), memory_space=pltpu.SMEM),
            pl.BlockSpec((tb, SUBLANES, LANES), lambda i: (i, 0, 0)),
            _const_spec(tab.shape),
        ],
        out_specs=pl.BlockSpec((tb, SUBLANES, LANES), lambda i: (i, 0, 0)),
        out_shape=jax.ShapeDtypeStruct((t, SUBLANES, LANES), F32),
        compiler_params=pltpu.CompilerParams(
            dimension_semantics=("arbitrary",), vmem_limit_bytes=VMEM_LIMIT),
        name="peer_out",
    )(offs, w, x3, tab)


def _peer_layer(x, g, w_query, sub_keys, u_tab, v_tab):
    b, s, d = x.shape
    t = b * s
    offs, gates, xlo, xhi = _peer_route(x.reshape(t, d), g, w_query, sub_keys)
    shape3 = (t, SUBLANES, d // SUBLANES)
    w = _peer_act(offs, xlo.reshape(shape3), xhi.reshape(shape3), gates, _pack_table(u_tab))
    y = _peer_out(offs, w, x.reshape(shape3), _pack_table(v_tab))
    return y.reshape(b, s, d)


def _norm_kernel(x_ref, g_ref, o_ref):
    o_ref[...] = _rmsnorm(x_ref[...], g_ref[...])


def _final_norm(x, g):
    b, s, d = x.shape
    t = b * s
    tm = 512
    out = pl.pallas_call(
        _norm_kernel,
        grid=(t // tm,),
        in_specs=[pl.BlockSpec((tm, d), lambda i: (i, 0)), _const_spec((1, d))],
        out_specs=pl.BlockSpec((tm, d), lambda i: (i, 0)),
        out_shape=jax.ShapeDtypeStruct((t, d), F32),
        name="final_norm",
    )(x.reshape(t, d), g.reshape(1, d))
    return out.reshape(b, s, d)


def kernel(x, norm_mix_g, gla_w_in, gla_w_gate_up, gla_b_gate, gla_g_head, gla_w_out, mlstm_w_in, mlstm_b_i, mlstm_b_f, mlstm_conv_w, mlstm_conv_b, mlstm_w_q, mlstm_w_k, mlstm_w_v, mlstm_skip, mlstm_g_head, mlstm_w_out, norm_ffn_g, peer_w_query, peer_sub_keys, peer_u, peer_v, norm_final_g):
    depth = norm_mix_g.shape[0]
    for i in range(depth):
        j = i // 2
        if i % 2 == 0:
            x = _gla_layer(x, norm_mix_g[i], gla_w_in[j], gla_w_gate_up[j], gla_b_gate[j],
                           gla_g_head[j], gla_w_out[j])
        else:
            x = _mlstm_layer(x, norm_mix_g[i], mlstm_w_in[j], mlstm_b_i[j], mlstm_b_f[j],
                             mlstm_conv_w[j], mlstm_conv_b[j], mlstm_w_q[j], mlstm_w_k[j],
                             mlstm_w_v[j], mlstm_skip[j], mlstm_g_head[j], mlstm_w_out[j])
        x = _peer_layer(x, norm_ffn_g[i], peer_w_query[i], peer_sub_keys[i], peer_u[i], peer_v[i])
    return _final_norm(x, norm_final_g)
```

```python
import functools
import math

import jax
import jax.numpy as jnp
from jax import lax
from jax.experimental import pallas as pl
from jax.experimental.pallas import tpu as pltpu

F32 = jnp.float32
BF16 = jnp.bfloat16

EPS = 1e-6
CHUNK = 64
LANES = 128
SUBLANES = 8

GLA_HEADS = 4
GLA_GATE_RANK = 16
GLA_GATE_TAU = 16.0

MLSTM_HEADS = 4
CONV_W = 4
QKV_BLOCK = 4
BD_TILE = 256

PEER_HEADS = 8
N_KEYS = 128
PEER_TOPK = 16
ROWS_PER_EXPERT = 4

MIX_TM = 256
ROUTE_TM = 256
PEER_TB = 64
VMEM_LIMIT = 56 * 1024 * 1024


def _split3(x):
    h1 = x.astype(BF16)
    r1 = x - h1.astype(F32)
    h2 = r1.astype(BF16)
    h3 = (r1 - h2.astype(F32)).astype(BF16)
    return h1, h2, h3


def _dot(a, b):
    return jnp.dot(a, b, preferred_element_type=F32)


def _dot_nt(a, b):
    return lax.dot_general(a, b, (((1,), (1,)), ((), ())), preferred_element_type=F32)


def _dot_tn(a, b):
    return lax.dot_general(a, b, (((0,), (0,)), ((), ())), preferred_element_type=F32)


def _exact_left_dot(m_bf16, x):
    h1, h2, h3 = _split3(x)
    return _dot(m_bf16, h1) + _dot(m_bf16, h2) + _dot(m_bf16, h3)


def _rmsnorm(x, g):
    return x * lax.rsqrt(jnp.mean(x * x, axis=-1, keepdims=True) + EPS) * g


def _log_sigmoid(z):
    return jnp.minimum(z, 0.0) - jnp.log1p(jnp.exp(-jnp.abs(z)))


def _silu(z):
    return z * (1.0 / (1.0 + jnp.exp(-z)))


def _const_spec(shape):
    nd = len(shape)
    return pl.BlockSpec(shape, lambda *_: (0,) * nd, pipeline_mode=pl.Buffered(1))


def _gla_kernel(x_ref, g_ref, wall_ref, wgu_ref, bg_ref, gh_ref, wout_ref, tri_ref,
                o_ref, st_ref, p_ref, la_ref, ob_ref, *, hqk, hv):
    dk = hqk // GLA_HEADS
    dv = hv // GLA_HEADS
    tm = x_ref.shape[1]

    @pl.when(pl.program_id(1) == 0)
    def _():
        st_ref[...] = jnp.zeros_like(st_ref)

    x = x_ref[0]
    xn = _rmsnorm(x, g_ref[...]).astype(BF16)
    p_ref[...] = _dot(xn, wall_ref[...])

    o_q, o_k, o_v, o_r, o_g = 0, hqk, 2 * hqk, 2 * hqk + hv, 2 * hqk + 2 * hv
    glow = p_ref[:, o_g:o_g + LANES]
    g1, g2, g3 = _split3(glow)
    w1, w2, w3 = wgu_ref[0], wgu_ref[1], wgu_ref[2]
    z = (_dot(g1, w1) + _dot(g1, w2) + _dot(g2, w1)
         + _dot(g1, w3) + _dot(g2, w2) + _dot(g3, w1)) + bg_ref[...]
    la_ref[...] = _log_sigmoid(z) * (1.0 / GLA_GATE_TAU)

    tri = tri_ref[...]
    for c in range(tm // CHUNK):
        r0 = c * CHUNK
        for h in range(GLA_HEADS):
            la = la_ref[r0:r0 + CHUNK, h * dk:(h + 1) * dk]
            cum = _exact_left_dot(tri, la)
            tot = cum[CHUNK - 1:CHUNK, :]
            kd = p_ref[r0:r0 + CHUNK, o_k + h * dk:o_k + (h + 1) * dk] * jnp.exp(tot - cum)
            vc = p_ref[r0:r0 + CHUNK, o_v + h * dv:o_v + (h + 1) * dv]
            st = st_ref[h] * jnp.exp(tot) + _dot_tn(vc.astype(BF16), kd.astype(BF16))
            st_ref[h] = st
            qc = p_ref[r0:r0 + CHUNK, o_q + h * dk:o_q + (h + 1) * dk] * (dk ** -0.5)
            ob_ref[r0:r0 + CHUNK, h * dv:(h + 1) * dv] = _dot_nt(qc.astype(BF16), st.astype(BF16))

    for h in range(GLA_HEADS):
        o = ob_ref[:, h * dv:(h + 1) * dv]
        o = o * lax.rsqrt(jnp.mean(o * o, axis=-1, keepdims=True) + EPS)
        r = p_ref[:, o_r + h * dv:o_r + (h + 1) * dv]
        ob_ref[:, h * dv:(h + 1) * dv] = o * gh_ref[:, h * dv:(h + 1) * dv] * _silu(r)
    o_ref[0] = x + _dot(ob_ref[...].astype(BF16), wout_ref[...])


def _gla_layer(x, g_mix, w_in, w_gate_up, b_gate, g_head, w_out):
    b, s, d = x.shape
    hqk = w_gate_up.shape[1]
    hv = g_head.shape[0]
    pad = LANES - GLA_GATE_RANK
    wall = jnp.pad(w_in, ((0, 0), (0, pad))).astype(BF16)
    wgu = jnp.stack(_split3(jnp.pad(w_gate_up, ((0, pad), (0, 0)))))
    tri = jnp.tril(jnp.ones((CHUNK, CHUNK), F32)).astype(BF16)
    ncol = wall.shape[1]
    tm = MIX_TM
    kern = functools.partial(_gla_kernel, hqk=hqk, hv=hv)
    return pl.pallas_call(
        kern,
        grid=(b, s // tm),
        in_specs=[
            pl.BlockSpec((1, tm, d), lambda i, j: (i, j, 0)),
            _const_spec((1, d)),
            _const_spec((d, ncol)),
            _const_spec((3, LANES, hqk)),
            _const_spec((1, hqk)),
            _const_spec((1, hv)),
            _const_spec((hv, d)),
            _const_spec((CHUNK, CHUNK)),
        ],
        out_specs=pl.BlockSpec((1, tm, d), lambda i, j: (i, j, 0)),
        out_shape=jax.ShapeDtypeStruct((b, s, d), F32),
        scratch_shapes=[
            pltpu.VMEM((GLA_HEADS, hv // GLA_HEADS, hqk // GLA_HEADS), F32),
            pltpu.VMEM((tm, ncol), F32),
            pltpu.VMEM((tm, hqk), F32),
            pltpu.VMEM((tm, hv), F32),
        ],
        compiler_params=pltpu.CompilerParams(
            dimension_semantics=("arbitrary", "arbitrary"), vmem_limit_bytes=VMEM_LIMIT),
        name="gla_layer",
    )(x, g_mix.reshape(1, d), wall, wgu, b_gate.reshape(1, hqk), g_head.reshape(1, hv),
      w_out.astype(BF16), tri)


def _mlstm_kernel(x_ref, g_ref, wall_ref, bi_ref, bf_ref, cw_ref, cb_ref, wq_ref, wk_ref, wv_ref,
                  skip_ref, gh_ref, wout_ref, tri_ref,
                  o_ref, ct_ref, n_ref, m_ref, tail_ref, p_ref, xc_ref, q_ref, k_ref, v_ref, hb_ref,
                  *, inner):
    dh = inner // MLSTM_HEADS
    tm = x_ref.shape[1]

    @pl.when(pl.program_id(1) == 0)
    def _():
        ct_ref[...] = jnp.zeros_like(ct_ref)
        n_ref[...] = jnp.zeros_like(n_ref)
        m_ref[...] = jnp.zeros_like(m_ref)
        tail_ref[...] = jnp.zeros_like(tail_ref)

    x = x_ref[0]
    xn = _rmsnorm(x, g_ref[...]).astype(BF16)
    p_ref[...] = _dot(xn, wall_ref[...])

    xm = p_ref[:, 0:inner]
    ext = jnp.concatenate([tail_ref[...], xm], axis=0)
    conv = cb_ref[...]
    for w in range(CONV_W):
        s0 = SUBLANES - (CONV_W - 1) + w
        conv = conv + cw_ref[w:w + 1, :] * ext[s0:s0 + tm, :]
    tail_ref[...] = xm[tm - SUBLANES:tm, :]
    xc = _silu(conv)
    xc_ref[...] = xc

    xcb = xc.astype(BF16)
    xmb = xm.astype(BF16)
    for t in range(inner // BD_TILE):
        sl = slice(t * BD_TILE, (t + 1) * BD_TILE)
        q_ref[:, sl] = _dot(xcb[:, sl], wq_ref[t])
        k_ref[:, sl] = _dot(xcb[:, sl], wk_ref[t]) * (dh ** -0.5)
        v_ref[:, sl] = _dot(xmb[:, sl], wv_ref[t]).astype(BF16)

    o_i = 2 * inner
    o_f = 2 * inner + LANES
    tri = tri_ref[...]
    for c in range(tm // CHUNK):
        r0 = c * CHUNK
        i_log = p_ref[r0:r0 + CHUNK, o_i:o_i + LANES] + bi_ref[...]
        log_f = _log_sigmoid(p_ref[r0:r0 + CHUNK, o_f:o_f + LANES] + bf_ref[...])
        cumf = _exact_left_dot(tri, log_f)
        tot = cumf[CHUNK - 1:CHUNK, :]
        w_log = tot - cumf + i_log
        m_old = m_ref[...]
        m_new = jnp.maximum(tot + m_old, jnp.max(w_log, axis=0, keepdims=True))
        a_prev = jnp.exp(tot + m_old - m_new)
        ew = jnp.exp(w_log - m_new)
        floor = jnp.exp(-m_new)
        m_ref[...] = m_new
        for h in range(MLSTM_HEADS):
            hs = slice(h * dh, (h + 1) * dh)
            a_h = a_prev[:, h:h + 1]
            kw = k_ref[r0:r0 + CHUNK, hs] * ew[:, h:h + 1]
            ct = a_h * ct_ref[h] + _dot_tn(v_ref[r0:r0 + CHUNK, hs], kw.astype(BF16))
            ct_ref[h] = ct
            n_new = a_h * n_ref[h] + jnp.sum(kw, axis=0, keepdims=True)
            n_ref[h] = n_new
            qc = q_ref[r0:r0 + CHUNK, hs]
            num = _dot_nt(qc.astype(BF16), ct.astype(BF16))
            den = jnp.maximum(jnp.abs(jnp.sum(qc * n_new, axis=-1, keepdims=True)), floor[:, h:h + 1])
            hb_ref[r0:r0 + CHUNK, hs] = num / den

    for h in range(MLSTM_HEADS):
        hs = slice(h * dh, (h + 1) * dh)
        hv = hb_ref[:, hs]
        mu = jnp.mean(hv, axis=-1, keepdims=True)
        cen = hv - mu
        var = jnp.mean(cen * cen, axis=-1, keepdims=True)
        hn = cen * lax.rsqrt(var + EPS) * gh_ref[:, hs]
        z = p_ref[:, inner + h * dh:inner + (h + 1) * dh]
        hb_ref[:, hs] = (hn + skip_ref[:, hs] * xc_ref[:, hs]) * _silu(z)
    o_ref[0] = x + _dot(hb_ref[...].astype(BF16), wout_ref[...])


def _blockdiag_tiles(w):
    nb, bs, _ = w.shape
    per = BD_TILE // bs
    wt = w.reshape(nb // per, per, bs, bs)
    eye = jnp.eye(per, dtype=w.dtype)
    full = wt[:, :, :, None, :] * eye[None, :, None, :, None]
    return full.reshape(nb // per, BD_TILE, BD_TILE).astype(BF16)


def _mlstm_layer(x, g_mix, w_in, b_i, b_f, conv_w, conv_b, w_q, w_k, w_v, skip, g_head, w_out):
    b, s, d = x.shape
    inner = conv_w.shape[1]
    nh = MLSTM_HEADS
    pad = LANES - nh
    wall = jnp.concatenate([
        w_in[:, :2 * inner],
        jnp.pad(w_in[:, 2 * inner:2 * inner + nh], ((0, 0), (0, pad))),
        jnp.pad(w_in[:, 2 * inner + nh:], ((0, 0), (0, pad))),
    ], axis=1).astype(BF16)
    ncol = wall.shape[1]
    bi = jnp.pad(b_i, (0, pad)).reshape(1, LANES)
    bf = jnp.pad(b_f, (0, pad)).reshape(1, LANES)
    tri = jnp.tril(jnp.ones((CHUNK, CHUNK), F32)).astype(BF16)
    nt = inner // BD_TILE
    tm = MIX_TM
    dh = inner // nh
    kern = functools.partial(_mlstm_kernel, inner=inner)
    return pl.pallas_call(
        kern,
        grid=(b, s // tm),
        in_specs=[
            pl.BlockSpec((1, tm, d), lambda i, j: (i, j, 0)),
            _const_spec((1, d)),
            _const_spec((d, ncol)),
            _const_spec((1, LANES)),
            _const_spec((1, LANES)),
            _const_spec((CONV_W, inner)),
            _const_spec((1, inner)),
            _const_spec((nt, BD_TILE, BD_TILE)),
            _const_spec((nt, BD_TILE, BD_TILE)),
            _const_spec((nt, BD_TILE, BD_TILE)),
            _const_spec((1, inner)),
            _const_spec((1, inner)),
            _const_spec((inner, d)),
            _const_spec((CHUNK, CHUNK)),
        ],
        out_specs=pl.BlockSpec((1, tm, d), lambda i, j: (i, j, 0)),
        out_shape=jax.ShapeDtypeStruct((b, s, d), F32),
        scratch_shapes=[
            pltpu.VMEM((nh, dh, dh), F32),
            pltpu.VMEM((nh, 1, dh), F32),
            pltpu.VMEM((1, LANES), F32),
            pltpu.VMEM((SUBLANES, inner), F32),
            pltpu.VMEM((tm, ncol), F32),
            pltpu.VMEM((tm, inner), F32),
            pltpu.VMEM((tm, inner), F32),
            pltpu.VMEM((tm, inner), F32),
            pltpu.VMEM((tm, inner), BF16),
            pltpu.VMEM((tm, inner), F32),
        ],
        compiler_params=pltpu.CompilerParams(
            dimension_semantics=("arbitrary", "arbitrary"), vmem_limit_bytes=VMEM_LIMIT),
        name="mlstm_layer",
    )(x, g_mix.reshape(1, d), wall, bi, bf, conv_w, conv_b.reshape(1, inner),
      _blockdiag_tiles(w_q), _blockdiag_tiles(w_k), _blockdiag_tiles(w_v),
      skip.reshape(1, inner), g_head.reshape(1, inner), w_out.astype(BF16), tri)


_NEG_INF = float("-inf")
_BIG = 3e38


def _topk_axis0(s, k, row_id, out_row):
    n = s.shape[0]
    vals = jnp.zeros(out_row.shape, F32)
    idxs = jnp.zeros(out_row.shape, F32)
    for i in range(k):
        m = jnp.max(s, axis=0, keepdims=True)
        j = jnp.min(jnp.where(s == m, row_id, float(n)), axis=0, keepdims=True)
        s = jnp.where(row_id == j, _NEG_INF, s)
        vals = jnp.where(out_row == i, m, vals)
        idxs = jnp.where(out_row == i, j, idxs)
    return vals, idxs


def _route_kernel(x_ref, g_ref, wqt_ref, keys_ref, offs_ref, gates_ref, xlo_ref, xhi_ref,
                  qt_ref, ex_ref, gt_ref):
    tm, d = x_ref.shape
    half = d // 2
    nk = N_KEYS
    kk = PEER_TOPK
    ne = float(nk * nk)
    xn = _rmsnorm(x_ref[...], g_ref[...])
    xlo_ref[...] = jnp.concatenate([xn[:, :half], xn[:, :half]], axis=1)
    xhi_ref[...] = jnp.concatenate([xn[:, half:], xn[:, half:]], axis=1)
    qt_ref[...] = _dot_nt(wqt_ref[...], xn.astype(BF16))

    key_id = lax.broadcasted_iota(jnp.int32, (nk, LANES), 0).astype(F32)
    rank = lax.broadcasted_iota(jnp.int32, (kk, LANES), 0)
    sub8 = lax.broadcasted_iota(jnp.int32, (SUBLANES, LANES), 0).astype(F32)

    def head(h, carry):
        for lb in range(tm // LANES):
            ls = slice(lb * LANES, (lb + 1) * LANES)
            tops = []
            for p in range(2):
                hp = 2 * h + p
                qs = qt_ref[pl.ds(pl.multiple_of(hp * nk, nk), nk), ls]
                s = _dot(keys_ref[hp], qs.astype(BF16))
                tops.append(_topk_axis0(s, kk, key_id, rank))
            (ta, ia), (tb, ib) = tops
            cands, keys = [], []
            for i in range(SUBLANES):
                cands.append(ta[i:i + 1] + tb[0:SUBLANES])
                keys.append((sub8 + float(i * kk)) * ne + ia[i:i + 1] * float(nk) + ib[0:SUBLANES])
            cands.append(ta[0:1] + tb[SUBLANES:kk])
            keys.append((sub8 + float(SUBLANES)) * ne + ia[0:1] * float(nk) + ib[SUBLANES:kk])
            cands.append(ta[SUBLANES:kk] + tb[0:1])
            keys.append((sub8 + float(SUBLANES)) * (kk * ne) + ia[SUBLANES:kk] * float(nk) + ib[0:1])
            cand = jnp.concatenate(cands, axis=0)
            key = jnp.concatenate(keys, axis=0)
            ex = jnp.zeros((kk, LANES), F32)
            ee = jnp.zeros((kk, LANES), F32)
            ssum = jnp.zeros((1, LANES), F32)
            m0 = None
            for i in range(kk):
                m = jnp.max(cand, axis=0, keepdims=True)
                sel = jnp.min(jnp.where(cand == m, key, _BIG), axis=0, keepdims=True)
                cand = jnp.where(key == sel, _NEG_INF, cand)
                if i == 0:
                    m0 = m
                e = jnp.exp(m - m0)
                expert = sel - jnp.floor(sel * (1.0 / ne)) * ne
                ex = jnp.where(rank == i, expert, ex)
                ee = jnp.where(rank == i, e, ee)
                ssum = ssum + e
            rows = pl.ds(pl.multiple_of(h * kk, kk), kk)
            ex_ref[rows, ls] = ex
            gt_ref[rows, ls] = ee / ssum
        return carry

    lax.fori_loop(0, PEER_HEADS, head, 0)

    lane = lax.broadcasted_iota(jnp.int32, (tm, LANES), 1)
    adj = jnp.where((lane % SUBLANES) < ROWS_PER_EXPERT, ROWS_PER_EXPERT, 0)
    offs_ref[...] = ex_ref[...].T.astype(jnp.int32) * ROWS_PER_EXPERT + adj
    gates_ref[...] = gt_ref[...].T


def _peer_route(x2, g, w_query, sub_keys):
    t, d = x2.shape
    nq = w_query.shape[1]
    keys = sub_keys.reshape(PEER_HEADS * 2, N_KEYS, -1).astype(BF16)
    tm = ROUTE_TM
    return pl.pallas_call(
        _route_kernel,
        grid=(t // tm,),
        in_specs=[
            pl.BlockSpec((tm, d), lambda i: (i, 0)),
            _const_spec((1, d)),
            _const_spec((nq, d)),
            _const_spec(keys.shape),
        ],
        out_specs=[
            pl.BlockSpec((tm, LANES), lambda i: (i, 0)),
            pl.BlockSpec((tm, LANES), lambda i: (i, 0)),
            pl.BlockSpec((tm, d), lambda i: (i, 0)),
            pl.BlockSpec((tm, d), lambda i: (i, 0)),
        ],
        out_shape=[
            jax.ShapeDtypeStruct((t, LANES), jnp.int32),
            jax.ShapeDtypeStruct((t, LANES), F32),
            jax.ShapeDtypeStruct((t, d), F32),
            jax.ShapeDtypeStruct((t, d), F32),
        ],
        scratch_shapes=[
            pltpu.VMEM((nq, tm), F32),
            pltpu.VMEM((LANES, tm), F32),
            pltpu.VMEM((LANES, tm), F32),
        ],
        compiler_params=pltpu.CompilerParams(
            dimension_semantics=("arbitrary",), vmem_limit_bytes=VMEM_LIMIT),
        name="peer_route",
    )(x2, g.reshape(1, d), w_query.T.astype(BF16), keys)


def _pack_table(tab):
    e, d = tab.shape
    bits = lax.bitcast_convert_type(tab.astype(BF16), jnp.uint16).astype(jnp.uint32)
    words = (bits[:, d // 2:] << 16) | bits[:, :d // 2]
    words = words.reshape(e * ROWS_PER_EXPERT, LANES)
    return jnp.pad(words, ((ROWS_PER_EXPERT, ROWS_PER_EXPERT), (0, 0)))


def _load_pair(tab_ref, off_a, off_b, lo_half):
    ta = tab_ref[pl.ds(off_a, SUBLANES), :]
    tb = tab_ref[pl.ds(off_b, SUBLANES), :]
    mt = jnp.where(lo_half, ta, tb)
    lo = pltpu.bitcast(mt << 16, F32)
    hi = pltpu.bitcast(mt & jnp.uint32(0xFFFF0000), F32)
    return lo, hi


_PAIR_SLOTS = ((0, 4), (2, 6), (1, 5), (3, 7))


def _act_kernel(offs_ref, xlo_ref, xhi_ref, gates_ref, tab_ref, w_ref, act_ref):
    tb = xlo_ref.shape[0]
    nslot = gates_ref.shape[1]
    sub = lax.broadcasted_iota(jnp.int32, (SUBLANES, LANES), 0)
    lo_half = sub < ROWS_PER_EXPERT
    m2 = (sub % 4) < 2
    m1 = (sub % 2) == 0

    def fold2(a, b_):
        return jnp.where(m2, a + pltpu.roll(a, 6, 0), b_ + pltpu.roll(b_, 2, 0))

    def fold1(a, b_):
        return jnp.where(m1, a + pltpu.roll(a, 7, 0), b_ + pltpu.roll(b_, 1, 0))

    def lane_sums(rows):
        return jnp.sum(jnp.concatenate(rows, axis=0).T, axis=0, keepdims=True)

    def token(t, prev_rows):
        act_ref[pl.ds(jnp.maximum(t - 1, 0), 1), :] = lane_sums(prev_rows)
        xlo = xlo_ref[t]
        xhi = xhi_ref[t]
        orow = offs_ref.at[t]
        rows = []
        for g in range(nslot // SUBLANES):
            prods = []
            for sa, sb in _PAIR_SLOTS:
                lo, hi = _load_pair(tab_ref, orow[g * SUBLANES + sa],
                                    orow[g * SUBLANES + sb], lo_half)
                prods.append(lo * xlo + hi * xhi)
            rows.append(fold1(fold2(prods[0], prods[1]), fold2(prods[2], prods[3])))
        return tuple(rows)

    zeros = tuple(jnp.zeros((SUBLANES, LANES), F32) for _ in range(nslot // SUBLANES))
    last_rows = lax.fori_loop(0, tb, token, zeros)
    act_ref[pl.ds(tb - 1, 1), :] = lane_sums(last_rows)

    act = act_ref[...]
    gelu = 0.5 * act * (1.0 + lax.erf(act * (1.0 / math.sqrt(2.0))))
    w_ref[...] = gates_ref[...] * gelu


def _peer_act(offs, xlo, xhi, gates, tab):
    t = offs.shape[0]
    tb = PEER_TB
    return pl.pallas_call(
        _act_kernel,
        grid=(t // tb,),
        in_specs=[
            pl.BlockSpec((tb, LANES), lambda i: (i, 0), memory_space=pltpu.SMEM),
            pl.BlockSpec((tb, SUBLANES, LANES), lambda i: (i, 0, 0)),
            pl.BlockSpec((tb, SUBLANES, LANES), lambda i: (i, 0, 0)),
            pl.BlockSpec((tb, LANES), lambda i: (i, 0)),
            _const_spec(tab.shape),
        ],
        out_specs=pl.BlockSpec((tb, LANES), lambda i: (i, 0)),
        out_shape=jax.ShapeDtypeStruct((t, LANES), F32),
        scratch_shapes=[pltpu.VMEM((tb, LANES), F32)],
        compiler_params=pltpu.CompilerParams(
            dimension_semantics=("arbitrary",), vmem_limit_bytes=VMEM_LIMIT),
        name="peer_act",
    )(offs, xlo, xhi, gates, tab)


def _out_kernel(offs_ref, w_ref, x_ref, tab_ref, o_ref, wva_ref, wvb_ref):
    tb = x_ref.shape[0]
    nslot = w_ref.shape[1]
    sub = lax.broadcasted_iota(jnp.int32, (SUBLANES, LANES), 0)
    lo_half = sub < ROWS_PER_EXPERT

    def splat(t, dst_ref):
        dst_ref[...] = jnp.broadcast_to(w_ref[pl.ds(t, 1), :], (LANES, nslot)).T

    def gather(t, wv_ref):
        orow = offs_ref.at[t]
        acc_lo = [jnp.zeros((SUBLANES, LANES), F32) for _ in range(2)]
        acc_hi = [jnp.zeros((SUBLANES, LANES), F32) for _ in range(2)]
        n = 0
        for g in range(nslot // SUBLANES):
            for sa, sb in _PAIR_SLOTS:
                ja = g * SUBLANES + sa
                jb = g * SUBLANES + sb
                lo, hi = _load_pair(tab_ref, orow[ja], orow[jb], lo_half)
                wv = jnp.where(lo_half,
                               jnp.broadcast_to(wv_ref[ja:ja + 1, :], (SUBLANES, LANES)),
                               jnp.broadcast_to(wv_ref[jb:jb + 1, :], (SUBLANES, LANES)))
                acc_lo[n % 2] = acc_lo[n % 2] + lo * wv
                acc_hi[n % 2] = acc_hi[n % 2] + hi * wv
                n += 1
        ylo = acc_lo[0] + acc_lo[1]
        yhi = acc_hi[0] + acc_hi[1]
        y = jnp.where(lo_half, ylo + pltpu.roll(ylo, 4, 0), yhi + pltpu.roll(yhi, 4, 0))
        o_ref[t] = x_ref[t] + y

    splat(0, wva_ref)

    def two_tokens(i, carry):
        t0 = 2 * i
        gather(t0, wva_ref)
        splat(t0 + 1, wvb_ref)
        gather(t0 + 1, wvb_ref)
        splat(jnp.minimum(t0 + 2, tb - 1), wva_ref)
        return carry

    lax.fori_loop(0, tb // 2, two_tokens, 0)


def _peer_out(offs, w, x3, tab):
    t = offs.shape[0]
    tb = PEER_TB
    return pl.pallas_call(
        _out_kernel,
        grid=(t // tb,),
        in_specs=[
            pl.BlockSpec((tb, LANES), lambda i: (i, 0), memory_space=pltpu.SMEM),
            pl.BlockSpec((tb, LANES), lambda i: (i, 0)),
            pl.BlockSpec((tb, SUBLANES, LANES), lambda i: (i, 0, 0)),
            _const_spec(tab.shape),
        ],
        out_specs=pl.BlockSpec((tb, SUBLANES, LANES), lambda i: (i, 0, 0)),
        out_shape=jax.ShapeDtypeStruct((t, SUBLANES, LANES), F32),
        scratch_shapes=[pltpu.VMEM((LANES, LANES), F32), pltpu.VMEM((LANES, LANES), F32)],
        compiler_params=pltpu.CompilerParams(
            dimension_semantics=("arbitrary",), vmem_limit_bytes=VMEM_LIMIT),
        name="peer_out",
    )(offs, w, x3, tab)


def _peer_layer(x, g, w_query, sub_keys, u_tab, v_tab):
    b, s, d = x.shape
    t = b * s
    offs, gates, xlo, xhi = _peer_route(x.reshape(t, d), g, w_query, sub_keys)
    shape3 = (t, SUBLANES, d // SUBLANES)
    w = _peer_act(offs, xlo.reshape(shape3), xhi.reshape(shape3), gates, _pack_table(u_tab))
    y = _peer_out(offs, w, x.reshape(shape3), _pack_table(v_tab))
    return y.reshape(b, s, d)


def _norm_kernel(x_ref, g_ref, o_ref):
    o_ref[...] = _rmsnorm(x_ref[...], g_ref[...])


def _final_norm(x, g):
    b, s, d = x.shape
    t = b * s
    tm = 512
    out = pl.pallas_call(
        _norm_kernel,
        grid=(t // tm,),
        in_specs=[pl.BlockSpec((tm, d), lambda i: (i, 0)), _const_spec((1, d))],
        out_specs=pl.BlockSpec((tm, d), lambda i: (i, 0)),
        out_shape=jax.ShapeDtypeStruct((t, d), F32),
        name="final_norm",
    )(x.reshape(t, d), g.reshape(1, d))
    return out.reshape(b, s, d)


def kernel(x, norm_mix_g, gla_w_in, gla_w_gate_up, gla_b_gate, gla_g_head, gla_w_out, mlstm_w_in, mlstm_b_i, mlstm_b_f, mlstm_conv_w, mlstm_conv_b, mlstm_w_q, mlstm_w_k, mlstm_w_v, mlstm_skip, mlstm_g_head, mlstm_w_out, norm_ffn_g, peer_w_query, peer_sub_keys, peer_u, peer_v, norm_final_g):
    depth = norm_mix_g.shape[0]
    for i in range(depth):
        j = i // 2
        if i % 2 == 0:
            x = _gla_layer(x, norm_mix_g[i], gla_w_in[j], gla_w_gate_up[j], gla_b_gate[j],
                           gla_g_head[j], gla_w_out[j])
        else:
            x = _mlstm_layer(x, norm_mix_g[i], mlstm_w_in[j], mlstm_b_i[j], mlstm_b_f[j],
                             mlstm_conv_w[j], mlstm_conv_b[j], mlstm_w_q[j], mlstm_w_k[j],
                             mlstm_w_v[j], mlstm_skip[j], mlstm_g_head[j], mlstm_w_out[j])
        x = _peer_layer(x, norm_ffn_g[i], peer_w_query[i], peer_sub_keys[i], peer_u[i], peer_v[i])
    return _final_norm(x, norm_final_g)
```

```python
import functools
import math

import jax
import jax.numpy as jnp
from jax import lax
from jax.experimental import pallas as pl
from jax.experimental.pallas import tpu as pltpu

F32 = jnp.float32
BF16 = jnp.bfloat16

EPS = 1e-6
CHUNK = 64
LANES = 128
SUBLANES = 8

GLA_HEADS = 4
GLA_GATE_RANK = 16
GLA_GATE_TAU = 16.0

MLSTM_HEADS = 4
CONV_W = 4
QKV_BLOCK = 4
BD_TILE = 256

PEER_HEADS = 8
N_KEYS = 128
PEER_TOPK = 16
ROWS_PER_EXPERT = 4

MIX_TM = 256
ROUTE_TM = 256
PEER_TB = 128
VMEM_LIMIT = 56 * 1024 * 1024


def _split3(x):
    h1 = x.astype(BF16)
    r1 = x - h1.astype(F32)
    h2 = r1.astype(BF16)
    h3 = (r1 - h2.astype(F32)).astype(BF16)
    return h1, h2, h3


def _dot(a, b):
    return jnp.dot(a, b, preferred_element_type=F32)


def _dot_nt(a, b):
    return lax.dot_general(a, b, (((1,), (1,)), ((), ())), preferred_element_type=F32)


def _dot_tn(a, b):
    return lax.dot_general(a, b, (((0,), (0,)), ((), ())), preferred_element_type=F32)


def _exact_left_dot(m_bf16, x):
    h1, h2, h3 = _split3(x)
    return _dot(m_bf16, h1) + _dot(m_bf16, h2) + _dot(m_bf16, h3)


def _rmsnorm(x, g):
    return x * lax.rsqrt(jnp.mean(x * x, axis=-1, keepdims=True) + EPS) * g


def _log_sigmoid(z):
    return jnp.minimum(z, 0.0) - jnp.log1p(jnp.exp(-jnp.abs(z)))


def _silu(z):
    return z * (1.0 / (1.0 + jnp.exp(-z)))


def _const_spec(shape):
    nd = len(shape)
    return pl.BlockSpec(shape, lambda *_: (0,) * nd, pipeline_mode=pl.Buffered(1))


def _gla_kernel(x_ref, g_ref, wall_ref, wgu_ref, bg_ref, gh_ref, wout_ref, tri_ref,
                o_ref, st_ref, p_ref, la_ref, ob_ref, *, hqk, hv):
    dk = hqk // GLA_HEADS
    dv = hv // GLA_HEADS
    tm = x_ref.shape[1]

    @pl.when(pl.program_id(1) == 0)
    def _():
        st_ref[...] = jnp.zeros_like(st_ref)

    x = x_ref[0]
    xn = _rmsnorm(x, g_ref[...]).astype(BF16)
    p_ref[...] = _dot(xn, wall_ref[...])

    o_q, o_k, o_v, o_r, o_g = 0, hqk, 2 * hqk, 2 * hqk + hv, 2 * hqk + 2 * hv
    glow = p_ref[:, o_g:o_g + LANES]
    g1, g2, g3 = _split3(glow)
    w1, w2, w3 = wgu_ref[0], wgu_ref[1], wgu_ref[2]
    z = (_dot(g1, w1) + _dot(g1, w2) + _dot(g2, w1)
         + _dot(g1, w3) + _dot(g2, w2) + _dot(g3, w1)) + bg_ref[...]
    la_ref[...] = _log_sigmoid(z) * (1.0 / GLA_GATE_TAU)

    tri = tri_ref[...]
    for c in range(tm // CHUNK):
        r0 = c * CHUNK
        for h in range(GLA_HEADS):
            la = la_ref[r0:r0 + CHUNK, h * dk:(h + 1) * dk]
            cum = _exact_left_dot(tri, la)
            tot = cum[CHUNK - 1:CHUNK, :]
            kd = p_ref[r0:r0 + CHUNK, o_k + h * dk:o_k + (h + 1) * dk] * jnp.exp(tot - cum)
            vc = p_ref[r0:r0 + CHUNK, o_v + h * dv:o_v + (h + 1) * dv]
            st = st_ref[h] * jnp.exp(tot) + _dot_tn(vc.astype(BF16), kd.astype(BF16))
            st_ref[h] = st
            qc = p_ref[r0:r0 + CHUNK, o_q + h * dk:o_q + (h + 1) * dk] * (dk ** -0.5)
            ob_ref[r0:r0 + CHUNK, h * dv:(h + 1) * dv] = _dot_nt(qc.astype(BF16), st.astype(BF16))

    for h in range(GLA_HEADS):
        o = ob_ref[:, h * dv:(h + 1) * dv]
        o = o * lax.rsqrt(jnp.mean(o * o, axis=-1, keepdims=True) + EPS)
        r = p_ref[:, o_r + h * dv:o_r + (h + 1) * dv]
        ob_ref[:, h * dv:(h + 1) * dv] = o * gh_ref[:, h * dv:(h + 1) * dv] * _silu(r)
    o_ref[0] = x + _dot(ob_ref[...].astype(BF16), wout_ref[...])


def _gla_layer(x, g_mix, w_in, w_gate_up, b_gate, g_head, w_out):
    b, s, d = x.shape
    hqk = w_gate_up.shape[1]
    hv = g_head.shape[0]
    pad = LANES - GLA_GATE_RANK
    wall = jnp.pad(w_in, ((0, 0), (0, pad))).astype(BF16)
    wgu = jnp.stack(_split3(jnp.pad(w_gate_up, ((0, pad), (0, 0)))))
    tri = jnp.tril(jnp.ones((CHUNK, CHUNK), F32)).astype(BF16)
    ncol = wall.shape[1]
    tm = MIX_TM
    kern = functools.partial(_gla_kernel, hqk=hqk, hv=hv)
    return pl.pallas_call(
        kern,
        grid=(b, s // tm),
        in_specs=[
            pl.BlockSpec((1, tm, d), lambda i, j: (i, j, 0)),
            _const_spec((1, d)),
            _const_spec((d, ncol)),
            _const_spec((3, LANES, hqk)),
            _const_spec((1, hqk)),
            _const_spec((1, hv)),
            _const_spec((hv, d)),
            _const_spec((CHUNK, CHUNK)),
        ],
        out_specs=pl.BlockSpec((1, tm, d), lambda i, j: (i, j, 0)),
        out_shape=jax.ShapeDtypeStruct((b, s, d), F32),
        scratch_shapes=[
            pltpu.VMEM((GLA_HEADS, hv // GLA_HEADS, hqk // GLA_HEADS), F32),
            pltpu.VMEM((tm, ncol), F32),
            pltpu.VMEM((tm, hqk), F32),
            pltpu.VMEM((tm, hv), F32),
        ],
        compiler_params=pltpu.CompilerParams(
            dimension_semantics=("arbitrary", "arbitrary"), vmem_limit_bytes=VMEM_LIMIT),
        name="gla_layer",
    )(x, g_mix.reshape(1, d), wall, wgu, b_gate.reshape(1, hqk), g_head.reshape(1, hv),
      w_out.astype(BF16), tri)


def _mlstm_kernel(x_ref, g_ref, wall_ref, bi_ref, bf_ref, cw_ref, cb_ref, wq_ref, wk_ref, wv_ref,
                  skip_ref, gh_ref, wout_ref, tri_ref,
                  o_ref, ct_ref, n_ref, m_ref, tail_ref, p_ref, xc_ref, q_ref, k_ref, v_ref, hb_ref,
                  *, inner):
    dh = inner // MLSTM_HEADS
    tm = x_ref.shape[1]

    @pl.when(pl.program_id(1) == 0)
    def _():
        ct_ref[...] = jnp.zeros_like(ct_ref)
        n_ref[...] = jnp.zeros_like(n_ref)
        m_ref[...] = jnp.zeros_like(m_ref)
        tail_ref[...] = jnp.zeros_like(tail_ref)

    x = x_ref[0]
    xn = _rmsnorm(x, g_ref[...]).astype(BF16)
    p_ref[...] = _dot(xn, wall_ref[...])

    xm = p_ref[:, 0:inner]
    ext = jnp.concatenate([tail_ref[...], xm], axis=0)
    conv = cb_ref[...]
    for w in range(CONV_W):
        s0 = SUBLANES - (CONV_W - 1) + w
        conv = conv + cw_ref[w:w + 1, :] * ext[s0:s0 + tm, :]
    tail_ref[...] = xm[tm - SUBLANES:tm, :]
    xc = _silu(conv)
    xc_ref[...] = xc

    xcb = xc.astype(BF16)
    xmb = xm.astype(BF16)
    for t in range(inner // BD_TILE):
        sl = slice(t * BD_TILE, (t + 1) * BD_TILE)
        q_ref[:, sl] = _dot(xcb[:, sl], wq_ref[t])
        k_ref[:, sl] = _dot(xcb[:, sl], wk_ref[t]) * (dh ** -0.5)
        v_ref[:, sl] = _dot(xmb[:, sl], wv_ref[t]).astype(BF16)

    o_i = 2 * inner
    o_f = 2 * inner + LANES
    tri = tri_ref[...]
    for c in range(tm // CHUNK):
        r0 = c * CHUNK
        i_log = p_ref[r0:r0 + CHUNK, o_i:o_i + LANES] + bi_ref[...]
        log_f = _log_sigmoid(p_ref[r0:r0 + CHUNK, o_f:o_f + LANES] + bf_ref[...])
        cumf = _exact_left_dot(tri, log_f)
        tot = cumf[CHUNK - 1:CHUNK, :]
        w_log = tot - cumf + i_log
        m_old = m_ref[...]
        m_new = jnp.maximum(tot + m_old, jnp.max(w_log, axis=0, keepdims=True))
        a_prev = jnp.exp(tot + m_old - m_new)
        ew = jnp.exp(w_log - m_new)
        floor = jnp.exp(-m_new)
        m_ref[...] = m_new
        for h in range(MLSTM_HEADS):
            hs = slice(h * dh, (h + 1) * dh)
            a_h = a_prev[:, h:h + 1]
            kw = k_ref[r0:r0 + CHUNK, hs] * ew[:, h:h + 1]
            ct = a_h * ct_ref[h] + _dot_tn(v_ref[r0:r0 + CHUNK, hs], kw.astype(BF16))
            ct_ref[h] = ct
            n_new = a_h * n_ref[h] + jnp.sum(kw, axis=0, keepdims=True)
            n_ref[h] = n_new
            qc = q_ref[r0:r0 + CHUNK, hs]
            num = _dot_nt(qc.astype(BF16), ct.astype(BF16))
            den = jnp.maximum(jnp.abs(jnp.sum(qc * n_new, axis=-1, keepdims=True)), floor[:, h:h + 1])
            hb_ref[r0:r0 + CHUNK, hs] = num / den

    for h in range(MLSTM_HEADS):
        hs = slice(h * dh, (h + 1) * dh)
        hv = hb_ref[:, hs]
        mu = jnp.mean(hv, axis=-1, keepdims=True)
        cen = hv - mu
        var = jnp.mean(cen * cen, axis=-1, keepdims=True)
        hn = cen * lax.rsqrt(var + EPS) * gh_ref[:, hs]
        z = p_ref[:, inner + h * dh:inner + (h + 1) * dh]
        hb_ref[:, hs] = (hn + skip_ref[:, hs] * xc_ref[:, hs]) * _silu(z)
    o_ref[0] = x + _dot(hb_ref[...].astype(BF16), wout_ref[...])


def _blockdiag_tiles(w):
    nb, bs, _ = w.shape
    per = BD_TILE // bs
    wt = w.reshape(nb // per, per, bs, bs)
    eye = jnp.eye(per, dtype=w.dtype)
    full = wt[:, :, :, None, :] * eye[None, :, None, :, None]
    return full.reshape(nb // per, BD_TILE, BD_TILE).astype(BF16)


def _mlstm_layer(x, g_mix, w_in, b_i, b_f, conv_w, conv_b, w_q, w_k, w_v, skip, g_head, w_out):
    b, s, d = x.shape
    inner = conv_w.shape[1]
    nh = MLSTM_HEADS
    pad = LANES - nh
    wall = jnp.concatenate([
        w_in[:, :2 * inner],
        jnp.pad(w_in[:, 2 * inner:2 * inner + nh], ((0, 0), (0, pad))),
        jnp.pad(w_in[:, 2 * inner + nh:], ((0, 0), (0, pad))),
    ], axis=1).astype(BF16)
    ncol = wall.shape[1]
    bi = jnp.pad(b_i, (0, pad)).reshape(1, LANES)
    bf = jnp.pad(b_f, (0, pad)).reshape(1, LANES)
    tri = jnp.tril(jnp.ones((CHUNK, CHUNK), F32)).astype(BF16)
    nt = inner // BD_TILE
    tm = MIX_TM
    dh = inner // nh
    kern = functools.partial(_mlstm_kernel, inner=inner)
    return pl.pallas_call(
        kern,
        grid=(b, s // tm),
        in_specs=[
            pl.BlockSpec((1, tm, d), lambda i, j: (i, j, 0)),
            _const_spec((1, d)),
            _const_spec((d, ncol)),
            _const_spec((1, LANES)),
            _const_spec((1, LANES)),
            _const_spec((CONV_W, inner)),
            _const_spec((1, inner)),
            _const_spec((nt, BD_TILE, BD_TILE)),
            _const_spec((nt, BD_TILE, BD_TILE)),
            _const_spec((nt, BD_TILE, BD_TILE)),
            _const_spec((1, inner)),
            _const_spec((1, inner)),
            _const_spec((inner, d)),
            _const_spec((CHUNK, CHUNK)),
        ],
        out_specs=pl.BlockSpec((1, tm, d), lambda i, j: (i, j, 0)),
        out_shape=jax.ShapeDtypeStruct((b, s, d), F32),
        scratch_shapes=[
            pltpu.VMEM((nh, dh, dh), F32),
            pltpu.VMEM((nh, 1, dh), F32),
            pltpu.VMEM((1, LANES), F32),
            pltpu.VMEM((SUBLANES, inner), F32),
            pltpu.VMEM((tm, ncol), F32),
            pltpu.VMEM((tm, inner), F32),
            pltpu.VMEM((tm, inner), F32),
            pltpu.VMEM((tm, inner), F32),
            pltpu.VMEM((tm, inner), BF16),
            pltpu.VMEM((tm, inner), F32),
        ],
        compiler_params=pltpu.CompilerParams(
            dimension_semantics=("arbitrary", "arbitrary"), vmem_limit_bytes=VMEM_LIMIT),
        name="mlstm_layer",
    )(x, g_mix.reshape(1, d), wall, bi, bf, conv_w, conv_b.reshape(1, inner),
      _blockdiag_tiles(w_q), _blockdiag_tiles(w_k), _blockdiag_tiles(w_v),
      skip.reshape(1, inner), g_head.reshape(1, inner), w_out.astype(BF16), tri)


_NEG_INF = float("-inf")
_BIG = 3e38


def _topk_axis0(s, k, row_id, out_row):
    n = s.shape[0]
    vals = jnp.zeros(out_row.shape, F32)
    idxs = jnp.zeros(out_row.shape, F32)
    for i in range(k):
        m = jnp.max(s, axis=0, keepdims=True)
        j = jnp.min(jnp.where(s == m, row_id, float(n)), axis=0, keepdims=True)
        s = jnp.where(row_id == j, _NEG_INF, s)
        vals = jnp.where(out_row == i, m, vals)
        idxs = jnp.where(out_row == i, j, idxs)
    return vals, idxs


def _route_kernel(x_ref, g_ref, wqt_ref, keys_ref, offs_ref, gates_ref, xlo_ref, xhi_ref,
                  qt_ref, ex_ref, gt_ref):
    tm, d = x_ref.shape
    half = d // 2
    nk = N_KEYS
    kk = PEER_TOPK
    ne = float(nk * nk)
    xn = _rmsnorm(x_ref[...], g_ref[...])
    for s in range(SUBLANES):
        c0 = (s % ROWS_PER_EXPERT) * LANES
        xlo_ref[:, s, :] = xn[:, c0:c0 + LANES]
        xhi_ref[:, s, :] = xn[:, half + c0:half + c0 + LANES]
    qt_ref[...] = _dot_nt(wqt_ref[...], xn.astype(BF16))

    key_id = lax.broadcasted_iota(jnp.int32, (nk, LANES), 0).astype(F32)
    rank = lax.broadcasted_iota(jnp.int32, (kk, LANES), 0)
    sub8 = lax.broadcasted_iota(jnp.int32, (SUBLANES, LANES), 0).astype(F32)

    def head(h, carry):
        for lb in range(tm // LANES):
            ls = slice(lb * LANES, (lb + 1) * LANES)
            tops = []
            for p in range(2):
                hp = 2 * h + p
                qs = qt_ref[pl.ds(pl.multiple_of(hp * nk, nk), nk), ls]
                s = _dot(keys_ref[hp], qs.astype(BF16))
                tops.append(_topk_axis0(s, kk, key_id, rank))
            (ta, ia), (tb, ib) = tops
            cands, keys = [], []
            for i in range(SUBLANES):
                cands.append(ta[i:i + 1] + tb[0:SUBLANES])
                keys.append((sub8 + float(i * kk)) * ne + ia[i:i + 1] * float(nk) + ib[0:SUBLANES])
            cands.append(ta[0:1] + tb[SUBLANES:kk])
            keys.append((sub8 + float(SUBLANES)) * ne + ia[0:1] * float(nk) + ib[SUBLANES:kk])
            cands.append(ta[SUBLANES:kk] + tb[0:1])
            keys.append((sub8 + float(SUBLANES)) * (kk * ne) + ia[SUBLANES:kk] * float(nk) + ib[0:1])
            cand = jnp.concatenate(cands, axis=0)
            key = jnp.concatenate(keys, axis=0)
            ex = jnp.zeros((kk, LANES), F32)
            ee = jnp.zeros((kk, LANES), F32)
            ssum = jnp.zeros((1, LANES), F32)
            m0 = None
            for i in range(kk):
                m = jnp.max(cand, axis=0, keepdims=True)
                sel = jnp.min(jnp.where(cand == m, key, _BIG), axis=0, keepdims=True)
                cand = jnp.where(key == sel, _NEG_INF, cand)
                if i == 0:
                    m0 = m
                e = jnp.exp(m - m0)
                expert = sel - jnp.floor(sel * (1.0 / ne)) * ne
                ex = jnp.where(rank == i, expert, ex)
                ee = jnp.where(rank == i, e, ee)
                ssum = ssum + e
            rows = pl.ds(pl.multiple_of(h * kk, kk), kk)
            ex_ref[rows, ls] = ex
            gt_ref[rows, ls] = ee / ssum
        return carry

    lax.fori_loop(0, PEER_HEADS, head, 0)

    slot = lax.broadcasted_iota(jnp.int32, (LANES, tm), 0)
    adj = jnp.where((slot % SUBLANES) < ROWS_PER_EXPERT, ROWS_PER_EXPERT, 0)
    offs_ref[...] = ex_ref[...].astype(jnp.int32) * ROWS_PER_EXPERT + adj
    gates_ref[...] = gt_ref[...].T


def _peer_route(x2, g, w_query, sub_keys):
    t, d = x2.shape
    nq = w_query.shape[1]
    keys = sub_keys.reshape(PEER_HEADS * 2, N_KEYS, -1).astype(BF16)
    tm = ROUTE_TM
    return pl.pallas_call(
        _route_kernel,
        grid=(t // tm,),
        in_specs=[
            pl.BlockSpec((tm, d), lambda i: (i, 0)),
            _const_spec((1, d)),
            _const_spec((nq, d)),
            _const_spec(keys.shape),
        ],
        out_specs=[
            pl.BlockSpec((LANES, tm), lambda i: (0, i)),
            pl.BlockSpec((tm, LANES), lambda i: (i, 0)),
            pl.BlockSpec((tm, SUBLANES, d // SUBLANES), lambda i: (i, 0, 0)),
            pl.BlockSpec((tm, SUBLANES, d // SUBLANES), lambda i: (i, 0, 0)),
        ],
        out_shape=[
            jax.ShapeDtypeStruct((LANES, t), jnp.int32),
            jax.ShapeDtypeStruct((t, LANES), F32),
            jax.ShapeDtypeStruct((t, SUBLANES, d // SUBLANES), F32),
            jax.ShapeDtypeStruct((t, SUBLANES, d // SUBLANES), F32),
        ],
        scratch_shapes=[
            pltpu.VMEM((nq, tm), F32),
            pltpu.VMEM((LANES, tm), F32),
            pltpu.VMEM((LANES, tm), F32),
        ],
        compiler_params=pltpu.CompilerParams(
            dimension_semantics=("arbitrary",), vmem_limit_bytes=VMEM_LIMIT),
        name="peer_route",
    )(x2, g.reshape(1, d), w_query.T.astype(BF16), keys)


def _pack_kernel(tab_ref, o_ref):
    eb, d = tab_ref.shape
    bits = pltpu.bitcast(tab_ref[...].astype(BF16).astype(F32), jnp.uint32)
    words = bits[:, d // 2:] | (bits[:, :d // 2] >> 16)
    for s in range(ROWS_PER_EXPERT):
        o_ref[pl.ds(s, eb, stride=ROWS_PER_EXPERT), :] = words[:, s * LANES:(s + 1) * LANES]


def _pack_table(tab):
    e, d = tab.shape
    eb = 256
    words = pl.pallas_call(
        _pack_kernel,
        grid=(e // eb,),
        in_specs=[pl.BlockSpec((eb, d), lambda i: (i, 0))],
        out_specs=pl.BlockSpec((eb * ROWS_PER_EXPERT, LANES), lambda i: (i, 0)),
        out_shape=jax.ShapeDtypeStruct((e * ROWS_PER_EXPERT, LANES), jnp.uint32),
        name="pack_table",
    )(tab)
    return jnp.pad(words, ((ROWS_PER_EXPERT, ROWS_PER_EXPERT), (0, 0)))


def _load_pair(tab_ref, off_a, off_b, lo_half):
    ta = tab_ref[pl.ds(off_a, SUBLANES), :]
    tb = tab_ref[pl.ds(off_b, SUBLANES), :]
    mt = jnp.where(lo_half, ta, tb)
    lo = pltpu.bitcast(mt << 16, F32)
    hi = pltpu.bitcast(mt & jnp.uint32(0xFFFF0000), F32)
    return lo, hi


_PAIR_SLOTS = ((0, 4), (2, 6), (1, 5), (3, 7))


def _offsets_pipeline(offs_hbm, sm_ref, sem_ref, process):
    i = pl.program_id(0)
    tb = sm_ref.shape[2]

    def copy(block, slot):
        return pltpu.make_async_copy(offs_hbm.at[:, pl.ds(block * tb, tb)], sm_ref.at[slot],
                                     sem_ref.at[slot])

    @pl.when(i == 0)
    def _():
        copy(0, 0).start()

    copy(2 * i + 1, 1).start()
    copy(2 * i, 0).wait()
    process(sm_ref.at[0], 0)

    @pl.when(i + 1 < pl.num_programs(0))
    def _():
        copy(2 * i + 2, 0).start()

    copy(2 * i + 1, 1).wait()
    process(sm_ref.at[1], tb)


def _act_kernel(offs_hbm, xlo_ref, xhi_ref, gates_ref, tab_ref, w_ref, act_ref, sm_ref, sem_ref):
    tb = sm_ref.shape[2]
    nslot = gates_ref.shape[1]
    sub = lax.broadcasted_iota(jnp.int32, (SUBLANES, LANES), 0)
    lo_half = sub < ROWS_PER_EXPERT
    m2 = (sub % 4) < 2
    m1 = (sub % 2) == 0

    def fold2(a, b_):
        return jnp.where(m2, a + pltpu.roll(a, 6, 0), b_ + pltpu.roll(b_, 2, 0))

    def fold1(a, b_):
        return jnp.where(m1, a + pltpu.roll(a, 7, 0), b_ + pltpu.roll(b_, 1, 0))

    def lane_sums(rows):
        return jnp.sum(jnp.concatenate(rows, axis=0).T, axis=0, keepdims=True)

    def process(offs_ref, row0):
        def token(t, prev_rows):
            act_ref[pl.ds(row0 + jnp.maximum(t - 1, 0), 1), :] = lane_sums(prev_rows)
            xlo = xlo_ref[row0 + t]
            xhi = xhi_ref[row0 + t]
            rows = []
            for g in range(nslot // SUBLANES):
                prods = []
                for sa, sb in _PAIR_SLOTS:
                    lo, hi = _load_pair(tab_ref, offs_ref.at[g * SUBLANES + sa][t],
                                        offs_ref.at[g * SUBLANES + sb][t], lo_half)
                    prods.append(lo * xlo + hi * xhi)
                rows.append(fold1(fold2(prods[0], prods[1]), fold2(prods[2], prods[3])))
            return tuple(rows)

        zeros = tuple(jnp.zeros((SUBLANES, LANES), F32) for _ in range(nslot // SUBLANES))
        last_rows = lax.fori_loop(0, tb, token, zeros)
        act_ref[pl.ds(row0 + tb - 1, 1), :] = lane_sums(last_rows)

    _offsets_pipeline(offs_hbm, sm_ref, sem_ref, process)

    act = act_ref[...]
    gelu = 0.5 * act * (1.0 + lax.erf(act * (1.0 / math.sqrt(2.0))))
    w_ref[...] = gates_ref[...] * gelu


def _peer_act(offs, xlo, xhi, gates, tab):
    t = offs.shape[1]
    half = PEER_TB
    tb = 2 * half
    return pl.pallas_call(
        _act_kernel,
        grid=(t // tb,),
        in_specs=[
            pl.BlockSpec(memory_space=pl.ANY),
            pl.BlockSpec((tb, SUBLANES, LANES), lambda i: (i, 0, 0)),
            pl.BlockSpec((tb, SUBLANES, LANES), lambda i: (i, 0, 0)),
            pl.BlockSpec((tb, LANES), lambda i: (i, 0)),
            _const_spec(tab.shape),
        ],
        out_specs=pl.BlockSpec((tb, LANES), lambda i: (i, 0)),
        out_shape=jax.ShapeDtypeStruct((t, LANES), F32),
        scratch_shapes=[
            pltpu.VMEM((tb, LANES), F32),
            pltpu.SMEM((2, LANES, half), jnp.int32),
            pltpu.SemaphoreType.DMA((2,)),
        ],
        compiler_params=pltpu.CompilerParams(
            dimension_semantics=("arbitrary",), vmem_limit_bytes=VMEM_LIMIT),
        name="peer_act",
    )(offs, xlo, xhi, gates, tab)


def _out_kernel(offs_hbm, w_ref, x_ref, gfin_ref, tab_ref, o_ref, wva_ref, wvb_ref, y_ref,
                sm_ref, sem_ref, *, final_norm):
    tb = sm_ref.shape[2]
    nslot = w_ref.shape[1]
    sub = lax.broadcasted_iota(jnp.int32, (SUBLANES, LANES), 0)
    lo_half = sub < ROWS_PER_EXPERT

    def splat(row, dst_ref):
        dst_ref[...] = jnp.broadcast_to(w_ref[pl.ds(row, 1), :], (LANES, nslot)).T

    def process(offs_ref, row0):
        def gather(t, wv_ref):
            acc_lo = [jnp.zeros((SUBLANES, LANES), F32) for _ in range(2)]
            acc_hi = [jnp.zeros((SUBLANES, LANES), F32) for _ in range(2)]
            n = 0
            for g in range(nslot // SUBLANES):
                for sa, sb in _PAIR_SLOTS:
                    ja = g * SUBLANES + sa
                    jb = g * SUBLANES + sb
                    lo, hi = _load_pair(tab_ref, offs_ref.at[ja][t], offs_ref.at[jb][t], lo_half)
                    wv = jnp.where(lo_half,
                                   jnp.broadcast_to(wv_ref[ja:ja + 1, :], (SUBLANES, LANES)),
                                   jnp.broadcast_to(wv_ref[jb:jb + 1, :], (SUBLANES, LANES)))
                    acc_lo[n % 2] = acc_lo[n % 2] + lo * wv
                    acc_hi[n % 2] = acc_hi[n % 2] + hi * wv
                    n += 1
            ylo = acc_lo[0] + acc_lo[1]
            yhi = acc_hi[0] + acc_hi[1]
            y_ref[row0 + t] = jnp.where(lo_half, ylo + pltpu.roll(ylo, 4, 0),
                                        yhi + pltpu.roll(yhi, 4, 0))

        splat(row0, wva_ref)

        def two_tokens(i, carry):
            t0 = 2 * i
            gather(t0, wva_ref)
            splat(row0 + t0 + 1, wvb_ref)
            gather(t0 + 1, wvb_ref)
            splat(row0 + jnp.minimum(t0 + 2, tb - 1), wva_ref)
            return carry

        lax.fori_loop(0, tb // 2, two_tokens, 0)

    _offsets_pipeline(offs_hbm, sm_ref, sem_ref, process)

    y = jnp.concatenate([y_ref[:, s, :] for s in range(SUBLANES)], axis=1)
    out = x_ref[...] + y
    if final_norm:
        out = _rmsnorm(out, gfin_ref[...])
    o_ref[...] = out


def _peer_out(offs, w, x2, tab, g_final, final_norm):
    t, d = x2.shape
    half = PEER_TB
    tb = 2 * half
    kern = functools.partial(_out_kernel, final_norm=final_norm)
    return pl.pallas_call(
        kern,
        grid=(t // tb,),
        in_specs=[
            pl.BlockSpec(memory_space=pl.ANY),
            pl.BlockSpec((tb, LANES), lambda i: (i, 0)),
            pl.BlockSpec((tb, d), lambda i: (i, 0)),
            _const_spec((1, d)),
            _const_spec(tab.shape),
        ],
        out_specs=pl.BlockSpec((tb, d), lambda i: (i, 0)),
        out_shape=jax.ShapeDtypeStruct((t, d), F32),
        scratch_shapes=[
            pltpu.VMEM((LANES, LANES), F32),
            pltpu.VMEM((LANES, LANES), F32),
            pltpu.VMEM((tb, SUBLANES, d // SUBLANES), F32),
            pltpu.SMEM((2, LANES, half), jnp.int32),
            pltpu.SemaphoreType.DMA((2,)),
        ],
        compiler_params=pltpu.CompilerParams(
            dimension_semantics=("arbitrary",), vmem_limit_bytes=VMEM_LIMIT),
        name="peer_out",
    )(offs, w, x2, g_final.reshape(1, d), tab)


def _peer_layer(x, g, w_query, sub_keys, u_tab, v_tab, g_final, final_norm):
    b, s, d = x.shape
    t = b * s
    x2 = x.reshape(t, d)
    offs, gates, xlo, xhi = _peer_route(x2, g, w_query, sub_keys)
    w = _peer_act(offs, xlo, xhi, gates, _pack_table(u_tab))
    y = _peer_out(offs, w, x2, _pack_table(v_tab), g_final, final_norm)
    return y.reshape(b, s, d)


def kernel(x, norm_mix_g, gla_w_in, gla_w_gate_up, gla_b_gate, gla_g_head, gla_w_out, mlstm_w_in, mlstm_b_i, mlstm_b_f, mlstm_conv_w, mlstm_conv_b, mlstm_w_q, mlstm_w_k, mlstm_w_v, mlstm_skip, mlstm_g_head, mlstm_w_out, norm_ffn_g, peer_w_query, peer_sub_keys, peer_u, peer_v, norm_final_g):
    depth = norm_mix_g.shape[0]
    for i in range(depth):
        j = i // 2
        if i % 2 == 0:
            x = _gla_layer(x, norm_mix_g[i], gla_w_in[j], gla_w_gate_up[j], gla_b_gate[j],
                           gla_g_head[j], gla_w_out[j])
        else:
            x = _mlstm_layer(x, norm_mix_g[i], mlstm_w_in[j], mlstm_b_i[j], mlstm_b_f[j],
                             mlstm_conv_w[j], mlstm_conv_b[j], mlstm_w_q[j], mlstm_w_k[j],
                             mlstm_w_v[j], mlstm_skip[j], mlstm_g_head[j], mlstm_w_out[j])
        x = _peer_layer(x, norm_ffn_g[i], peer_w_query[i], peer_sub_keys[i], peer_u[i], peer_v[i],
                        norm_final_g, final_norm=(i == depth - 1))
    return x
```
